```python
import jax, jax.numpy as jnp
from jax import lax
import numpy as np

D_MODEL = 1024
BATCH = 16
SEQ = 2048
DEPTH = 4
DEC_BATCH = 128
DEC_SEQ = 8
PAST_LEN = 8192
PAGE_SIZE = 128

D_FF = 2816
CONV_WIDTH = 256
CONV_K = 3
M_HEADS = 4
M_DK = 64
M_DV = 64
M_WIDTH = M_HEADS * M_DV
MLSTM_CHUNK = 64
A_HEADS = 8
QK_NOPE = 64
QK_ROPE = 32
V_HEAD = 64
Q_LORA = 384
KV_LORA = 256
ROPE_BASE = 10000.0
ATTN_BLOCK = 128
EPS = 1e-6
IN_SPLITS = (CONV_WIDTH, CONV_WIDTH, CONV_WIDTH,
             M_HEADS * M_DK, M_HEADS * M_DK, M_WIDTH, M_WIDTH, M_HEADS, M_HEADS,
             Q_LORA, KV_LORA, QK_ROPE,
             D_MODEL, D_MODEL, D_MODEL)
N_IN = sum(IN_SPLITS)

kernel_name = 'hybrid_conv_mlstm_mla_macaron_adaln_step'


def rmsnorm(x, g):
    xf = x.astype(jnp.float32)
    y = xf * lax.rsqrt(jnp.mean(xf * xf, axis=-1, keepdims=True) + EPS)
    return (y * g.astype(jnp.float32)).astype(x.dtype)


def modulate(x, shift, scale):
    return x * (1 + scale) + shift


def swiglu(x, w13, w2):
    g, u = jnp.split(x @ w13, 2, axis=-1)
    return (jax.nn.silu(g) * u) @ w2


def rope(x, pos):
    half = QK_ROPE // 2
    inv = ROPE_BASE ** (-jnp.arange(half, dtype=jnp.float32) * 2.0 / QK_ROPE)
    ang = pos.astype(jnp.float32)[:, None] * inv[None, :]
    ang = ang.reshape(ang.shape[:1] + (1,) * (x.ndim - 3) + ang.shape[1:])
    cos, sin = jnp.cos(ang), jnp.sin(ang)
    xf = x.astype(jnp.float32)
    x1, x2 = xf[..., :half], xf[..., half:]
    return jnp.concatenate([x1 * cos - x2 * sin, x1 * sin + x2 * cos], axis=-1).astype(x.dtype)


def short_conv(cu, prev, w):
    T = cu.shape[1]
    cpad = jnp.concatenate([prev.astype(cu.dtype), cu], axis=1)
    out = w[0] * cpad[:, 0:T]
    for j in range(1, CONV_K):
        out = out + w[j] * cpad[:, j:j + T]
    return out, cpad[:, T:]


def mlstm_chunk_step(carry, inp):
    C, n, m = carry
    q, k, v, ig, lf = inp
    L = q.shape[2]
    b = jnp.cumsum(lf, axis=-1)
    causal = jnp.tril(jnp.ones((L, L), dtype=bool))
    dlog = jnp.where(causal, b[..., :, None] - b[..., None, :] + ig[..., None, :], -jnp.inf)
    inter = b + m[..., None]
    m_t = jnp.maximum(inter, jnp.max(dlog, axis=-1))
    w_inter = jnp.exp(inter - m_t)
    s = jnp.einsum('bhld,bhsd->bhls', q, k) * jnp.exp(dlog - m_t[..., None])
    num = w_inter[..., None] * jnp.einsum('bhld,bhde->bhle', q, C) + jnp.einsum('bhls,bhse->bhle', s, v)
    den = w_inter * jnp.einsum('bhld,bhd->bhl', q, n) + jnp.sum(s, axis=-1)
    h = num / jnp.maximum(jnp.abs(den), jnp.exp(-m_t))[..., None]
    bl = b[..., -1]
    glog = bl[..., None] - b + ig
    m_new = jnp.maximum(bl + m, jnp.max(glog, axis=-1))
    a = jnp.exp(bl + m - m_new)
    ws = jnp.exp(glog - m_new[..., None])
    C_new = a[..., None, None] * C + jnp.einsum('bhs,bhsd,bhse->bhde', ws, k, v)
    n_new = a[..., None] * n + jnp.einsum('bhs,bhsd->bhd', ws, k)
    return (C_new, n_new, m_new), h


def mlstm_scan(q, k, v, ig, lf, C0, n0, m0):
    B, H, T, _ = q.shape
    L = MLSTM_CHUNK if T % MLSTM_CHUNK == 0 else T
    nc = T // L

    def chunks(a):
        return jnp.moveaxis(a.reshape(a.shape[:2] + (nc, L) + a.shape[3:]), 2, 0)

    (C, n, m), hs = lax.scan(mlstm_chunk_step,
                             (C0.astype(jnp.float32), n0.astype(jnp.float32), m0.astype(jnp.float32)),
                             (chunks(q), chunks(k), chunks(v), chunks(ig), chunks(lf)))
    h = jnp.moveaxis(hs, 0, 2).reshape(B, H, T, -1)
    return h, C, n, m


def mla_attend(q_lat, q_rope, kv, kr, pos0):
    B, T = q_lat.shape[:2]
    Tk = kv.shape[1]
    qb = ATTN_BLOCK if T % ATTN_BLOCK == 0 else T
    nb = T // qb
    k_pos = jnp.arange(Tk)
    scale = (QK_NOPE + QK_ROPE) ** -0.5

    def block(args):
        ql, qr, qp = args
        s = (jnp.einsum('bqhc,bkc->bhqk', ql, kv) + jnp.einsum('bqhr,bkr->bhqk', qr, kr)).astype(jnp.float32) * scale
        s = jnp.where(k_pos[None, :] <= qp[:, None], s, -jnp.inf)
        p = jax.nn.softmax(s, axis=-1).astype(kv.dtype)
        return jnp.einsum('bhqk,bkc->bqhc', p, kv)

    def split(a):
        return jnp.moveaxis(a.reshape((B, nb, qb) + a.shape[2:]), 1, 0)

    q_pos = (pos0 + jnp.arange(T)).reshape(nb, qb)
    o = lax.map(block, (split(q_lat), split(q_rope), q_pos))
    return jnp.moveaxis(o, 0, 1).reshape(B, T, A_HEADS, KV_LORA)


def layer_forward(x, c, p, conv_prev, C0, n0, m0, past_kv, past_kr, pos0):
    B, T = x.shape[:2]
    pos = pos0 + jnp.arange(T)
    mods = (jax.nn.silu(c) @ p['w_ada'] + p['b_ada'])[:, None, :]
    sh1, sc1, g1, sh2, sc2, g2, sh3, sc3, g3 = jnp.split(mods, 9, axis=-1)
    h = x + 0.5 * g1 * swiglu(modulate(rmsnorm(x, p['ln_ffn1']), sh1, sc1), p['ffn1_w13'], p['ffn1_w2'])
    u = modulate(rmsnorm(h, p['ln_mix']), sh2, sc2)
    z = u @ p['w_in']
    (zb, zc, zx, zq, zk, zv, zo, zi, zf, zdq, zdkv, zkr, gc, gm, ga) = jnp.split(
        z, np.cumsum(IN_SPLITS)[:-1].tolist(), axis=-1)
    conv, conv_new = short_conv(zc * zx, conv_prev, p['conv_w'])
    y_conv = (zb * conv) @ p['conv_out']
    def heads(a, d):
        return a.reshape(B, T, M_HEADS, d).transpose(0, 2, 1, 3).astype(jnp.float32)
    q = heads(zq, M_DK) * (M_DK ** -0.5)
    k = heads(zk, M_DK)
    v = heads(zv, M_DV)
    ig = (zi + p['mlstm_b_i']).astype(jnp.float32).transpose(0, 2, 1)
    lf = jax.nn.log_sigmoid((zf + p['mlstm_b_f']).astype(jnp.float32)).transpose(0, 2, 1)
    hm, C1, n1, m1 = mlstm_scan(q, k, v, ig, lf, C0, n0, m0)
    hm = hm * lax.rsqrt(jnp.mean(hm * hm, axis=-1, keepdims=True) + EPS)
    hm = hm.transpose(0, 2, 1, 3).reshape(B, T, M_WIDTH).astype(x.dtype)
    y_m = (hm * p['mlstm_norm'] * jax.nn.sigmoid(zo)) @ p['mlstm_out']
    cq = rmsnorm(zdq, p['mla_q_norm'])
    qa = (cq @ p['mla_w_uq']).reshape(B, T, A_HEADS, QK_NOPE + QK_ROPE)
    q_nope = qa[..., :QK_NOPE]
    q_rope = rope(qa[..., QK_NOPE:], pos)
    q_lat = jnp.einsum('bthn,chn->bthc', q_nope, p['mla_w_uk'].reshape(KV_LORA, A_HEADS, QK_NOPE))
    ckv = rmsnorm(zdkv, p['mla_kv_norm'])
    kr = rope(zkr, pos)
    if past_kv is None:
        kv_all, kr_all = ckv, kr
    else:
        kv_all = jnp.concatenate([past_kv.astype(ckv.dtype), ckv], axis=1)
        kr_all = jnp.concatenate([past_kr.astype(kr.dtype), kr], axis=1)
    o_lat = mla_attend(q_lat, q_rope, kv_all, kr_all, pos0)
    o = jnp.einsum('bthc,chv->bthv', o_lat, p['mla_w_uv'].reshape(KV_LORA, A_HEADS, V_HEAD))
    y_a = o.reshape(B, T, A_HEADS * V_HEAD) @ p['mla_w_o']
    merged = jax.nn.sigmoid(gc) * y_conv + jax.nn.sigmoid(gm) * y_m + jax.nn.sigmoid(ga) * y_a
    h = h + g2 * (merged @ p['w_mix_out'])
    out = h + 0.5 * g3 * swiglu(modulate(rmsnorm(h, p['ln_ffn2']), sh3, sc3), p['ffn2_w13'], p['ffn2_w2'])
    return out, (ckv, kr, conv_new, C1, n1, m1)


def setup_inputs(seed: int = 0) -> dict:
    key = jax.random.key(seed)
    keys = list(jax.random.split(key, 48))

    def nrm(shape, scale=1.0):
        return scale * jax.random.normal(keys.pop(), shape, jnp.float32)

    def gain(shape):
        return 1.0 + nrm(shape, 0.05)

    L, D = DEPTH, D_MODEL
    n_pages = PAST_LEN // PAGE_SIZE
    n_pool = (DEC_BATCH * n_pages * 5) // 4
    inp = {}
    inp['x_prompt'] = nrm((BATCH, SEQ, D))
    inp['x_sample'] = nrm((DEC_BATCH, DEC_SEQ, D))
    inp['cache_latent'] = nrm((L, n_pool, PAGE_SIZE, KV_LORA))
    inp['cache_krope'] = nrm((L, n_pool, PAGE_SIZE, QK_ROPE))
    inp['state_conv'] = nrm((L, DEC_BATCH, CONV_K - 1, CONV_WIDTH), 0.5)
    inp['state_mlstm_C'] = nrm((L, DEC_BATCH, M_HEADS, M_DK, M_DV))
    inp['state_mlstm_n'] = nrm((L, DEC_BATCH, M_HEADS, M_DK))
    inp['state_mlstm_m'] = nrm((L, DEC_BATCH, M_HEADS), 0.5)
    perm = jax.random.permutation(keys.pop(), n_pool)[:DEC_BATCH * n_pages]
    inp['page_table'] = perm.reshape(DEC_BATCH, n_pages).astype(jnp.int32)
    inp['c_prompt'] = nrm((BATCH, D))
    inp['c_sample'] = nrm((DEC_BATCH, D))
    inp['ln_ffn1'] = gain((L, D))
    inp['ffn1_w13'] = nrm((L, D, 2 * D_FF), D ** -0.5)
    inp['ffn1_w2'] = nrm((L, D_FF, D), D_FF ** -0.5)
    inp['ln_mix'] = gain((L, D))
    inp['w_in'] = nrm((L, D, N_IN), D ** -0.5)
    inp['conv_w'] = nrm((L, CONV_K, CONV_WIDTH), CONV_K ** -0.5)
    inp['conv_out'] = nrm((L, CONV_WIDTH, D), CONV_WIDTH ** -0.5)
    inp['mlstm_b_i'] = nrm((L, M_HEADS), 0.1)
    inp['mlstm_b_f'] = 3.0 + nrm((L, M_HEADS), 0.1)
    inp['mlstm_norm'] = gain((L, M_WIDTH))
    inp['mlstm_out'] = nrm((L, M_WIDTH, D), M_WIDTH ** -0.5)
    inp['mla_q_norm'] = gain((L, Q_LORA))
    inp['mla_w_uq'] = nrm((L, Q_LORA, A_HEADS * (QK_NOPE + QK_ROPE)), Q_LORA ** -0.5)
    inp['mla_w_uk'] = nrm((L, KV_LORA, A_HEADS * QK_NOPE), KV_LORA ** -0.5)
    inp['mla_kv_norm'] = gain((L, KV_LORA))
    inp['mla_w_uv'] = nrm((L, KV_LORA, A_HEADS * V_HEAD), KV_LORA ** -0.5)
    inp['mla_w_o'] = nrm((L, A_HEADS * V_HEAD, D), (A_HEADS * V_HEAD) ** -0.5)
    inp['w_mix_out'] = nrm((L, D, D), D ** -0.5)
    inp['ln_ffn2'] = gain((L, D))
    inp['ffn2_w13'] = nrm((L, D, 2 * D_FF), D ** -0.5)
    inp['ffn2_w2'] = nrm((L, D_FF, D), D_FF ** -0.5)
    inp['w_ada'] = nrm((L, D, 9 * D), 0.5 * D ** -0.5)
    inp['b_ada'] = nrm((L, 9 * D), 0.02)
    inp['final_norm'] = gain((D,))
    return inp


def reference(x_prompt, x_sample, cache_latent, cache_krope, state_conv, state_mlstm_C, state_mlstm_n,
              state_mlstm_m, page_table, c_prompt, c_sample, ln_ffn1, ffn1_w13, ffn1_w2, ln_mix, w_in,
              conv_w, conv_out, mlstm_b_i, mlstm_b_f, mlstm_norm, mlstm_out, mla_q_norm, mla_w_uq,
              mla_w_uk, mla_kv_norm, mla_w_uv, mla_w_o, w_mix_out, ln_ffn2, ffn2_w13, ffn2_w2,
              w_ada, b_ada, final_norm):
    bp = x_prompt.shape[0]
    bs = x_sample.shape[0]
    past_len = page_table.shape[1] * PAGE_SIZE
    xp, xs = x_prompt, x_sample
    p_st, s_st = [], []
    for l in range(DEPTH):
        p = {'w_ada': w_ada[l], 'b_ada': b_ada[l], 'ln_ffn1': ln_ffn1[l], 'ffn1_w13': ffn1_w13[l],
             'ffn1_w2': ffn1_w2[l], 'ln_mix': ln_mix[l], 'w_in': w_in[l], 'conv_w': conv_w[l],
             'conv_out': conv_out[l], 'mlstm_b_i': mlstm_b_i[l], 'mlstm_b_f': mlstm_b_f[l],
             'mlstm_norm': mlstm_norm[l], 'mlstm_out': mlstm_out[l], 'mla_q_norm': mla_q_norm[l],
             'mla_w_uq': mla_w_uq[l], 'mla_w_uk': mla_w_uk[l], 'mla_kv_norm': mla_kv_norm[l],
             'mla_w_uv': mla_w_uv[l], 'mla_w_o': mla_w_o[l], 'w_mix_out': w_mix_out[l],
             'ln_ffn2': ln_ffn2[l], 'ffn2_w13': ffn2_w13[l], 'ffn2_w2': ffn2_w2[l]}
        xp, sp = layer_forward(
            xp, c_prompt, p,
            jnp.zeros((bp, CONV_K - 1, CONV_WIDTH), xp.dtype),
            jnp.zeros((bp, M_HEADS, M_DK, M_DV), jnp.float32),
            jnp.zeros((bp, M_HEADS, M_DK), jnp.float32),
            jnp.zeros((bp, M_HEADS), jnp.float32),
            None, None, 0)
        p_st.append(sp)
        past_kv = cache_latent[l][page_table].reshape(bs, past_len, KV_LORA)
        past_kr = cache_krope[l][page_table].reshape(bs, past_len, QK_ROPE)
        xs, ss = layer_forward(xs, c_sample, p, state_conv[l], state_mlstm_C[l], state_mlstm_n[l],
                               state_mlstm_m[l], past_kv, past_kr, past_len)
        s_st.append(ss)
    y_prompt = rmsnorm(xp, final_norm)
    y_sample = rmsnorm(xs, final_norm)
    p_latent, p_krope, p_conv, p_C, p_n, p_m = (jnp.stack(a) for a in zip(*p_st))
    s_latent, s_krope, s_conv, s_C, s_n, s_m = (jnp.stack(a) for a in zip(*s_st))
    return (y_prompt, y_sample, p_latent, p_krope, p_conv, p_C, p_n, p_m,
            s_latent, s_krope, s_conv, s_C, s_n, s_m)
```

```python
import functools

import jax
import jax.numpy as jnp
import numpy as np
from jax import lax
from jax.experimental import pallas as pl
from jax.experimental.pallas import tpu as pltpu

D_MODEL = 1024
DEPTH = 4
PAGE_SIZE = 128
D_FF = 2816
CONV_WIDTH = 256
CONV_K = 3
M_HEADS = 4
M_DK = 64
M_DV = 64
M_WIDTH = M_HEADS * M_DV
A_HEADS = 8
QK_NOPE = 64
QK_ROPE = 32
V_HEAD = 64
Q_LORA = 384
KV_LORA = 256
ROPE_BASE = 10000.0
EPS = 1e-6

LANES = 128
VMEM_LIMIT = 56 * 1024 * 1024

F32 = jnp.float32
BF16 = jnp.bfloat16
NEG_INF = float("-inf")

ZA_W = 7 * 256
OFF_DQ = ZA_W
OFF_DKV = OFF_DQ + Q_LORA
OFF_GATES = OFF_DKV + KV_LORA
OFF_SMALL_A = OFF_GATES + 3 * D_MODEL
OFF_SMALL_B = OFF_SMALL_A + LANES
N_IN_R = OFF_SMALL_B + LANES
SM_IG = 0
SM_LF = M_HEADS
SM_KR = 64
QK_CAT = KV_LORA + LANES
Q_SLOT = LANES

FFN_TF = 1408
ATT_TQ = 128
ATT_TK = 512
PAGES_PER_STEP = 16
MLSTM_CHUNK_P = 256


def _cparams(n_axes):
    return pltpu.CompilerParams(dimension_semantics=("arbitrary",) * n_axes,
                                vmem_limit_bytes=VMEM_LIMIT)


def _bdot(a, b):
    return jnp.dot(a, b, preferred_element_type=F32)


def _bdot_nt(a, b):
    return lax.dot_general(a, b, (((1,), (1,)), ((), ())), preferred_element_type=F32)


def _rms(x):
    return x * lax.rsqrt(jnp.mean(x * x, axis=-1, keepdims=True) + EPS)


def _mods_kernel(c_ref, w_ref, b_ref, o_ref):
    c = c_ref[...]
    a = (c * jax.nn.sigmoid(c)).astype(BF16)
    o_ref[...] = _bdot(a, w_ref[...].astype(BF16)) + b_ref[...]


def _mods_call(c_all, w_ada, b_ada):
    nb = c_all.shape[0]
    return pl.pallas_call(
        _mods_kernel,
        out_shape=jax.ShapeDtypeStruct((DEPTH, 9, nb, D_MODEL), F32),
        grid=(DEPTH, 9),
        in_specs=[
            pl.BlockSpec((nb, D_MODEL), lambda l, j: (0, 0)),
            pl.BlockSpec((None, D_MODEL, D_MODEL), lambda l, j: (l, 0, j)),
            pl.BlockSpec((None, 1, D_MODEL), lambda l, j: (l, 0, j)),
        ],
        out_specs=pl.BlockSpec((None, None, nb, D_MODEL), lambda l, j: (l, j, 0, 0)),
        compiler_params=_cparams(2),
        name="adaln_mods",
    )(c_all, w_ada, b_ada.reshape(DEPTH, 1, 9 * D_MODEL))


def _vec_spec(layer, width):
    return pl.BlockSpec((None, 1, width), lambda *_: (layer, 0, 0))


def _mod_spec(layer, which, g):
    return pl.BlockSpec((None, None, g, 1, D_MODEL), lambda i, *_: (layer, which, i, 0, 0))


def _ffn_kernel(x_ref, ln_ref, sh_ref, sc_ref, g_ref, w1_ref, w3_ref, w2_ref, fn_ref, o_ref,
                xn_ref, acc_ref, *, final):
    k = pl.program_id(2)
    g_sz, tt, _ = x_ref.shape

    @pl.when(k == 0)
    def _():
        y = _rms(x_ref[...]) * ln_ref[...]
        u = y * (1 + sc_ref[...]) + sh_ref[...]
        xn_ref[...] = u.reshape(g_sz * tt, D_MODEL).astype(BF16)
        acc_ref[...] = jnp.zeros_like(acc_ref)

    xn = xn_ref[...]
    gate = _bdot(xn, w1_ref[...])
    up = _bdot(xn, w3_ref[...])
    act = (gate * jax.nn.sigmoid(gate) * up).astype(BF16)
    acc_ref[...] += _bdot(act, w2_ref[...])

    @pl.when(k == pl.num_programs(2) - 1)
    def _():
        out = x_ref[...] + (0.5 * g_ref[...]) * acc_ref[...].reshape(g_sz, tt, D_MODEL)
        if final:
            out = _rms(out) * fn_ref[...]
        o_ref[...] = out


def _ffn_call(x, mods, layer, first, ln, w13, w2, fnorm, final, g_sz, tt):
    nseq, t, _ = x.shape
    nk = D_FF // FFN_TF
    base = 0 if first else 6
    grid = (nseq // g_sz, t // tt, nk)
    row = pl.BlockSpec((g_sz, tt, D_MODEL), lambda i, j, k: (i, j, 0))
    return pl.pallas_call(
        functools.partial(_ffn_kernel, final=final),
        out_shape=jax.ShapeDtypeStruct(x.shape, F32),
        grid=grid,
        in_specs=[
            row,
            _vec_spec(layer, D_MODEL),
            _mod_spec(layer, base + 0, g_sz),
            _mod_spec(layer, base + 1, g_sz),
            _mod_spec(layer, base + 2, g_sz),
            pl.BlockSpec((None, D_MODEL, FFN_TF), lambda i, j, k: (layer, 0, k)),
            pl.BlockSpec((None, D_MODEL, FFN_TF), lambda i, j, k: (layer, 0, k + nk)),
            pl.BlockSpec((None, FFN_TF, D_MODEL), lambda i, j, k: (layer, k, 0)),
            pl.BlockSpec((1, D_MODEL), lambda i, j, k: (0, 0)),
        ],
        out_specs=row,
        scratch_shapes=[pltpu.VMEM((g_sz * tt, D_MODEL), BF16),
                        pltpu.VMEM((g_sz * tt, D_MODEL), F32)],
        compiler_params=_cparams(3),
        name="ffn",
    )(x, ln, mods, mods, mods, w13, w13, w2, fnorm)


def _inproj_kernel(h_ref, ln_ref, sh_ref, sc_ref, w_ref, qn_ref, kvn_ref, cos_ref, sin_ref, brow_ref,
                   za_ref, cq_ref, ckv_ref, kcat_ref, gates_ref, small_ref):
    g_sz, tt, _ = h_ref.shape
    m = g_sz * tt
    y = _rms(h_ref[...]) * ln_ref[...]
    u = y * (1 + sc_ref[...]) + sh_ref[...]
    xn = u.reshape(m, D_MODEL).astype(BF16)

    za_ref[...] = _bdot(xn, w_ref[:, 0:ZA_W])
    zdq = _bdot(xn, w_ref[:, OFF_DQ:OFF_DKV])
    cq_ref[...] = (_rms(zdq) * qn_ref[...]).astype(cq_ref.dtype)
    zdkv = _bdot(xn, w_ref[:, OFF_DKV:OFF_GATES])
    ckv = _rms(zdkv) * kvn_ref[...]
    ckv_ref[...] = ckv
    gates_ref[...] = jax.nn.sigmoid(_bdot(xn, w_ref[:, OFF_GATES:OFF_SMALL_A])).astype(gates_ref.dtype)

    sa = _bdot(xn, w_ref[:, OFF_SMALL_A:OFF_SMALL_B])
    sb = _bdot(xn, w_ref[:, OFF_SMALL_B:N_IN_R])
    rot = sa * cos_ref[...] + sb * sin_ref[...]
    lane = lax.broadcasted_iota(jnp.int32, rot.shape, 1)
    zb = rot + brow_ref[...]
    logsig = jnp.minimum(zb, 0.0) - jnp.log1p(jnp.exp(-jnp.abs(zb)))
    is_kr = (lane >= SM_KR) & (lane < SM_KR + QK_ROPE)
    small = jnp.where(lane < SM_LF, zb,
                      jnp.where(lane < SM_LF + M_HEADS, logsig,
                                jnp.where(is_kr, rot, 0.0)))
    small_ref[...] = small
    kcat_ref[:, 0:KV_LORA] = ckv.astype(kcat_ref.dtype)
    kcat_ref[:, KV_LORA:QK_CAT] = jnp.where(is_kr, rot, 0.0).astype(kcat_ref.dtype)


def _inproj_call(h, mods, layer, ln, w_in_r, qn, kvn, cos_t, sin_t, brow, g_sz, tt, act_dtype):
    nseq, t, _ = h.shape
    n = nseq * t
    m = g_sz * tt
    nj = t // tt
    grid = (nseq // g_sz, nj)
    rows = lambda w: pl.BlockSpec((m, w), lambda i, j: (i * nj + j, 0))
    out_shapes = (
        jax.ShapeDtypeStruct((n, ZA_W), F32),
        jax.ShapeDtypeStruct((n, Q_LORA), BF16),
        jax.ShapeDtypeStruct((n, KV_LORA), F32),
        jax.ShapeDtypeStruct((n, QK_CAT), act_dtype),
        jax.ShapeDtypeStruct((n, 3 * D_MODEL), F32),
        jax.ShapeDtypeStruct((n, LANES), F32),
    )
    return pl.pallas_call(
        _inproj_kernel,
        out_shape=out_shapes,
        grid=grid,
        in_specs=[
            pl.BlockSpec((g_sz, tt, D_MODEL), lambda i, j: (i, j, 0)),
            _vec_spec(layer, D_MODEL),
            _mod_spec(layer, 3, g_sz),
            _mod_spec(layer, 4, g_sz),
            pl.BlockSpec((None, D_MODEL, N_IN_R), lambda i, j: (layer, 0, 0)),
            _vec_spec(layer, Q_LORA),
            _vec_spec(layer, KV_LORA),
            pl.BlockSpec((m, LANES), lambda i, j: (j, 0)),
            pl.BlockSpec((m, LANES), lambda i, j: (j, 0)),
            _vec_spec(layer, LANES),
        ],
        out_specs=(rows(ZA_W), rows(Q_LORA), rows(KV_LORA), rows(QK_CAT), rows(3 * D_MODEL), rows(LANES)),
        compiler_params=_cparams(2),
        name="inproj",
    )(h, ln, mods, mods, w_in_r, qn, kvn, cos_t, sin_t, brow)


def _conv_kernel(zb_ref, zc_ref, zx_ref, prev_ref, w_ref, a_ref, st_ref, carry_ref):
    j = pl.program_id(1)
    g_sz, tt, c = zc_ref.shape

    @pl.when(j == 0)
    def _():
        carry_ref[...] = prev_ref[...]

    cu = zc_ref[...] * zx_ref[...]
    prev = carry_ref[...]
    t = lax.broadcasted_iota(jnp.int32, cu.shape, 1)
    p0 = prev[:, 0:1, :]
    p1 = prev[:, 1:2, :]
    c1 = jnp.where(t == 0, p1, pltpu.roll(cu, 1, axis=1))
    c2 = jnp.where(t == 0, p0, jnp.where(t == 1, p1, pltpu.roll(cu, 2, axis=1)))
    w = w_ref[...]
    conv = w[0:1, :] * c2 + w[1:2, :] * c1 + w[2:3, :] * cu
    a = zb_ref[...] * conv
    a_ref[...] = a.reshape(g_sz * tt, c).astype(a_ref.dtype)
    new = cu[:, tt - (CONV_K - 1):tt, :]
    carry_ref[...] = new
    st_ref[...] = new


def _conv_call(za3, prev, conv_w_l, g_sz, tt):
    nseq, t, _ = za3.shape
    nj = t // tt
    c = CONV_WIDTH
    col = lambda b: pl.BlockSpec((g_sz, tt, c), lambda i, j: (i, j, b))
    st = pl.BlockSpec((g_sz, CONV_K - 1, c), lambda i, j: (i, 0, 0))
    return pl.pallas_call(
        _conv_kernel,
        out_shape=(jax.ShapeDtypeStruct((nseq * t, c), BF16),
                   jax.ShapeDtypeStruct((nseq, CONV_K - 1, c), F32)),
        grid=(nseq // g_sz, nj),
        in_specs=[col(0), col(1), col(2), st,
                  pl.BlockSpec((CONV_K, c), lambda i, j: (0, 0))],
        out_specs=(pl.BlockSpec((g_sz * tt, c), lambda i, j: (i * nj + j, 0)), st),
        scratch_shapes=[pltpu.VMEM((g_sz, CONV_K - 1, c), F32)],
        compiler_params=_cparams(2),
        name="conv_mixer",
    )(za3, za3, za3, prev, conv_w_l)


def _mlstm_kernel(q_ref, k_ref, v_ref, zo_ref, sm_ref, gt_ref, c0_ref, n0_ref, m0_ref, nw_ref,
                  a_ref, c_ref, n_ref, m_ref, c_s, n_s, m_s):
    j = pl.program_id(1)
    lc = q_ref.shape[1]
    w = M_WIDTH

    @pl.when(j == 0)
    def _():
        c_s[...] = c0_ref[0]
        n_s[...] = n0_ref[0]
        m_s[...] = m0_ref[0]

    q = q_ref[0] * (M_DK ** -0.5)
    k = k_ref[0]
    v = v_ref[0]
    sm = sm_ref[0]
    gt = gt_ref[0]
    ri = lax.broadcasted_iota(jnp.int32, (lc, lc), 0)
    ci = lax.broadcasted_iota(jnp.int32, (lc, lc), 1)
    causal = ci <= ri
    bcol = jnp.dot(causal.astype(F32), sm, precision=lax.Precision.HIGHEST, preferred_element_type=F32)
    brow = jnp.dot(gt, (ri <= ci).astype(F32), precision=lax.Precision.HIGHEST, preferred_element_type=F32)

    lane = lax.broadcasted_iota(jnp.int32, (1, w), 1)
    lane_s = lax.broadcasted_iota(jnp.int32, (1, LANES), 1)
    cmat = c_s[...]
    nrow = n_s[...]
    mrow = m_s[...]
    kb = k.astype(BF16)
    vb = v.astype(BF16)
    qc = _bdot(q.astype(BF16), cmat.astype(BF16))

    zeros_l = jnp.zeros((lc, w), F32)
    num_intra = zeros_l
    den_l = zeros_l
    winter_l = zeros_l
    emt_l = zeros_l
    ws_l = zeros_l
    a_l = jnp.zeros((1, w), F32)
    m_new_row = mrow
    for h in range(M_HEADS):
        hm = (lane >= h * M_DV) & (lane < (h + 1) * M_DV)
        igc = sm[:, SM_IG + h:SM_IG + h + 1]
        bc = bcol[:, SM_LF + h:SM_LF + h + 1]
        igr = gt[SM_IG + h:SM_IG + h + 1, :]
        br = brow[SM_LF + h:SM_LF + h + 1, :]
        mp = mrow[:, h:h + 1]
        dlog = jnp.where(causal, bc - br + igr, NEG_INF)
        inter = bc + mp
        mt = jnp.maximum(inter, jnp.max(dlog, axis=1, keepdims=True))
        w_inter = jnp.exp(inter - mt)
        s = _bdot_nt(jnp.where(hm, q, 0.0).astype(BF16), kb) * jnp.exp(dlog - mt)
        num_intra = num_intra + _bdot(s.astype(BF16), jnp.where(hm, v, 0.0).astype(BF16))
        qn = jnp.sum(jnp.where(hm, q * nrow, 0.0), axis=1, keepdims=True)
        den = w_inter * qn + jnp.sum(s, axis=1, keepdims=True)
        den_l = jnp.where(hm, den, den_l)
        winter_l = jnp.where(hm, w_inter, winter_l)
        emt_l = jnp.where(hm, jnp.exp(-mt), emt_l)
        bl = bc[lc - 1:lc, :]
        glog = bl - bc + igc
        mn = jnp.maximum(bl + mp, jnp.max(glog, axis=0, keepdims=True))
        ws_l = jnp.where(hm, jnp.exp(glog - mn), ws_l)
        a_l = jnp.where(hm, jnp.exp(bl + mp - mn), a_l)
        m_new_row = jnp.where(lane_s == h, mn, m_new_row)

    num = winter_l * qc + num_intra
    hh = num / jnp.maximum(jnp.abs(den_l), emt_l)
    hsq = hh * hh
    ms_l = zeros_l
    for h in range(M_HEADS):
        hm = (lane >= h * M_DV) & (lane < (h + 1) * M_DV)
        ms = jnp.sum(jnp.where(hm, hsq, 0.0), axis=1, keepdims=True) * (1.0 / M_DV)
        ms_l = jnp.where(hm, ms, ms_l)
    hn = hh * lax.rsqrt(ms_l + EPS)
    a_ref[...] = (hn * nw_ref[...] * jax.nn.sigmoid(zo_ref[0])).astype(a_ref.dtype)

    kw = k * ws_l
    upd = lax.dot_general(kw.astype(BF16), vb, (((0,), (0,)), ((), ())), preferred_element_type=F32)
    r2 = lax.broadcasted_iota(jnp.int32, (w, w), 0)
    c2 = lax.broadcasted_iota(jnp.int32, (w, w), 1)
    same_head = (r2 // M_DK) == (c2 // M_DV)
    c_new = a_l * cmat + jnp.where(same_head, upd, 0.0)
    n_new = a_l * nrow + jnp.sum(kw, axis=0, keepdims=True)
    c_s[...] = c_new
    n_s[...] = n_new
    m_s[...] = m_new_row

    @pl.when(j == pl.num_programs(1) - 1)
    def _():
        c_ref[0] = c_new
        n_ref[0] = n_new
        m_ref[0] = m_new_row


def _mlstm_call(za3, small3, gates_t, c0, n0, m0, norm_w, lc, act_dtype):
    nseq, t, _ = za3.shape
    nj = t // lc
    w = M_WIDTH
    col = lambda b: pl.BlockSpec((1, lc, w), lambda i, j: (i, j, b))
    st_c = pl.BlockSpec((1, w, w), lambda i, j: (i, 0, 0))
    st_n = pl.BlockSpec((1, 1, w), lambda i, j: (i, 0, 0))
    st_m = pl.BlockSpec((1, 1, LANES), lambda i, j: (i, 0, 0))
    return pl.pallas_call(
        _mlstm_kernel,
        out_shape=(jax.ShapeDtypeStruct((nseq * t, w), act_dtype),
                   jax.ShapeDtypeStruct((nseq, w, w), F32),
                   jax.ShapeDtypeStruct((nseq, 1, w), F32),
                   jax.ShapeDtypeStruct((nseq, 1, LANES), F32)),
        grid=(nseq, nj),
        in_specs=[col(3), col(4), col(5), col(6),
                  pl.BlockSpec((1, lc, LANES), lambda i, j: (i, j, 0)),
                  pl.BlockSpec((1, 2 * M_HEADS, lc), lambda i, j: (i, 0, j)),
                  st_c, st_n, st_m,
                  pl.BlockSpec((1, w), lambda i, j: (0, 0))],
        out_specs=(pl.BlockSpec((lc, w), lambda i, j: (i * nj + j, 0)), st_c, st_n, st_m),
        scratch_shapes=[pltpu.VMEM((w, w), F32), pltpu.VMEM((1, w), F32), pltpu.VMEM((1, LANES), F32)],
        compiler_params=_cparams(2),
        name="mlstm",
    )(za3, za3, za3, za3, small3, gates_t, c0, n0, m0, norm_w)


def _qprep_kernel(cq_ref, wq_ref, wqs_ref, wuk_ref, cos_ref, sin_ref, q_ref):
    g_sz, _, tt, _ = q_ref.shape
    cq = cq_ref[...]
    qa = _bdot(cq, wq_ref[...])
    qs = _bdot(cq, wqs_ref[...])
    cos = cos_ref[...]
    sin = sin_ref[...]
    lane = lax.broadcasted_iota(jnp.int32, cos.shape, 1)
    is_rope = (lane >= SM_KR) & (lane < SM_KR + QK_ROPE)
    scale = (QK_NOPE + QK_ROPE) ** -0.5
    for h in range(A_HEADS):
        slot = qa[:, h * Q_SLOT:(h + 1) * Q_SLOT] * cos + qs[:, h * Q_SLOT:(h + 1) * Q_SLOT] * sin
        q_lat = _bdot(slot.astype(BF16), wuk_ref[h]) * scale
        q_rope = jnp.where(is_rope, slot, 0.0) * scale
        q_ref[:, h, :, 0:KV_LORA] = q_lat.reshape(g_sz, tt, KV_LORA).astype(q_ref.dtype)
        q_ref[:, h, :, KV_LORA:QK_CAT] = q_rope.reshape(g_sz, tt, LANES).astype(q_ref.dtype)


def _qprep_call(cq, wq, wqs, wuk, cos_t, sin_t, layer, nseq, t, g_sz, tt, q_dtype):
    m = g_sz * tt
    nj = t // tt
    qw = A_HEADS * Q_SLOT
    return pl.pallas_call(
        _qprep_kernel,
        out_shape=jax.ShapeDtypeStruct((nseq, A_HEADS, t, QK_CAT), q_dtype),
        grid=(nseq // g_sz, nj),
        in_specs=[
            pl.BlockSpec((m, Q_LORA), lambda i, j: (i * nj + j, 0)),
            pl.BlockSpec((None, Q_LORA, qw), lambda i, j: (layer, 0, 0)),
            pl.BlockSpec((None, Q_LORA, qw), lambda i, j: (layer, 0, 0)),
            pl.BlockSpec((None, A_HEADS, Q_SLOT, KV_LORA), lambda i, j: (layer, 0, 0, 0)),
            pl.BlockSpec((m, LANES), lambda i, j: (j, 0)),
            pl.BlockSpec((m, LANES), lambda i, j: (j, 0)),
        ],
        out_specs=pl.BlockSpec((g_sz, A_HEADS, tt, QK_CAT), lambda i, j: (i, 0, j, 0)),
        compiler_params=_cparams(2),
        name="mla_qprep",
    )(cq, wq, wqs, wuk, cos_t, sin_t)


def _softmax_step(s, kv, m_ref, l_ref, acc_ref):
    m_prev = m_ref[...]
    m_new = jnp.maximum(m_prev, jnp.max(s, axis=1, keepdims=True))
    alpha = jnp.exp(m_prev - m_new)
    p = jnp.exp(s - m_new)
    l_ref[...] = alpha * l_ref[...] + jnp.sum(p, axis=1, keepdims=True)
    acc_ref[...] = alpha * acc_ref[...] + _bdot(p.astype(BF16), kv)
    m_ref[...] = m_new


def _attn_prompt_kernel(q_ref, k_ref, o_ref, m_ref, l_ref, acc_ref):
    qi = pl.program_id(1)
    tq = q_ref.shape[2]
    rows = A_HEADS * tq
    q = q_ref[0].reshape(rows, QK_CAT)
    m_ref[...] = jnp.full_like(m_ref, NEG_INF)
    l_ref[...] = jnp.zeros_like(l_ref)
    acc_ref[...] = jnp.zeros_like(acc_ref)
    row = lax.broadcasted_iota(jnp.int32, (rows, ATT_TK), 0)
    col = lax.broadcasted_iota(jnp.int32, (rows, ATT_TK), 1)
    qpos = qi * tq + (row & (tq - 1))

    def body(kb, carry):
        k = k_ref[pl.ds(pl.multiple_of(kb * ATT_TK, ATT_TK), ATT_TK), :]
        s = _bdot_nt(q, k)
        s = jnp.where(kb * ATT_TK + col <= qpos, s, NEG_INF)
        _softmax_step(s, k[:, 0:KV_LORA], m_ref, l_ref, acc_ref)
        return carry

    n_kb = (qi * tq + tq + ATT_TK - 1) // ATT_TK
    lax.fori_loop(0, n_kb, body, 0)
    o_ref[0] = (acc_ref[...] / l_ref[...]).reshape(A_HEADS, tq, KV_LORA).astype(o_ref.dtype)


def _attn_prompt_call(q, kcat3):
    nseq, _, t, _ = q.shape
    tq = ATT_TQ
    rows = A_HEADS * tq
    return pl.pallas_call(
        _attn_prompt_kernel,
        out_shape=jax.ShapeDtypeStruct((nseq, A_HEADS, t, KV_LORA), BF16),
        grid=(nseq, t // tq),
        in_specs=[pl.BlockSpec((1, A_HEADS, tq, QK_CAT), lambda i, j: (i, 0, j, 0)),
                  pl.BlockSpec((None, t, QK_CAT), lambda i, j: (i, 0, 0))],
        out_specs=pl.BlockSpec((1, A_HEADS, tq, KV_LORA), lambda i, j: (i, 0, j, 0)),
        scratch_shapes=[pltpu.VMEM((rows, 1), F32), pltpu.VMEM((rows, 1), F32),
                        pltpu.VMEM((rows, KV_LORA), F32)],
        compiler_params=_cparams(2),
        name="attn_prompt",
    )(q, kcat3)


def _attn_sample_kernel(pt_ref, q_ref, knew_ref, *refs):
    del pt_ref
    pg = PAGES_PER_STEP
    lat_refs = refs[0:pg]
    kr_refs = refs[pg:2 * pg]
    o_ref = refs[2 * pg]
    kbuf, krbuf, m_ref, l_ref, acc_ref = refs[2 * pg + 1:]
    j = pl.program_id(1)
    dec = q_ref.shape[2]
    rows = A_HEADS * dec

    @pl.when(j == 0)
    def _():
        m_ref[...] = jnp.full_like(m_ref, NEG_INF)
        l_ref[...] = jnp.zeros_like(l_ref)
        acc_ref[...] = jnp.zeros_like(acc_ref)

    for i in range(pg):
        kbuf[i * PAGE_SIZE:(i + 1) * PAGE_SIZE, :] = lat_refs[i][...].astype(BF16)
        krbuf[i * PAGE_SIZE:(i + 1) * PAGE_SIZE, :] = kr_refs[i][...].astype(BF16)
    q = q_ref[0].reshape(rows, QK_CAT)
    q_lat = q[:, 0:KV_LORA].astype(BF16)
    q_rope = q[:, KV_LORA + SM_KR:KV_LORA + SM_KR + QK_ROPE].astype(BF16)
    kv = kbuf[...]
    s = _bdot_nt(q_lat, kv) + _bdot_nt(q_rope, krbuf[...])
    _softmax_step(s, kv, m_ref, l_ref, acc_ref)

    @pl.when(j == pl.num_programs(1) - 1)
    def _():
        knew = knew_ref[0].astype(BF16)
        s2 = _bdot_nt(q.astype(BF16), knew)
        row = lax.broadcasted_iota(jnp.int32, s2.shape, 0)
        col = lax.broadcasted_iota(jnp.int32, s2.shape, 1)
        s2 = jnp.where(col <= (row & (dec - 1)), s2, NEG_INF)
        _softmax_step(s2, knew[:, 0:KV_LORA], m_ref, l_ref, acc_ref)
        o_ref[0] = (acc_ref[...] / l_ref[...]).reshape(A_HEADS, dec, KV_LORA)


def _attn_sample_call(page_table, q, kcat3, cache_latent, cache_krope, layer):
    nseq, _, dec, _ = q.shape
    n_pages = page_table.shape[1]
    pg = PAGES_PER_STEP
    rows = A_HEADS * dec

    def page_spec(i, width):
        return pl.BlockSpec((None, None, PAGE_SIZE, width),
                            lambda s, j, pt: (layer, pt[s, j * pg + i], 0, 0))

    in_specs = [pl.BlockSpec((1, A_HEADS, dec, QK_CAT), lambda s, j, pt: (s, 0, 0, 0)),
                pl.BlockSpec((1, dec, QK_CAT), lambda s, j, pt: (s, 0, 0))]
    in_specs += [page_spec(i, KV_LORA) for i in range(pg)]
    in_specs += [page_spec(i, QK_ROPE) for i in range(pg)]
    grid_spec = pltpu.PrefetchScalarGridSpec(
        num_scalar_prefetch=1,
        grid=(nseq, n_pages // pg),
        in_specs=in_specs,
        out_specs=pl.BlockSpec((1, A_HEADS, dec, KV_LORA), lambda s, j, pt: (s, 0, 0, 0)),
        scratch_shapes=[pltpu.VMEM((pg * PAGE_SIZE, KV_LORA), BF16),
                        pltpu.VMEM((pg * PAGE_SIZE, QK_ROPE), BF16),
                        pltpu.VMEM((rows, 1), F32), pltpu.VMEM((rows, 1), F32),
                        pltpu.VMEM((rows, KV_LORA), F32)],
    )
    return pl.pallas_call(
        _attn_sample_kernel,
        out_shape=jax.ShapeDtypeStruct((nseq, A_HEADS, dec, KV_LORA), F32),
        grid_spec=grid_spec,
        compiler_params=_cparams(2),
        name="attn_sample",
    )(page_table, q, kcat3, *([cache_latent] * pg), *([cache_krope] * pg))


def _merge_kernel(h_ref, g2_ref, ac_ref, am_ref, ol_ref, gates_ref, wc_ref, wm_ref, wuv_ref, wo_ref, wmix_ref,
                  o_ref):
    g_sz, tt, _ = h_ref.shape
    m = g_sz * tt
    y_conv = _bdot(ac_ref[...].astype(BF16), wc_ref[...])
    y_m = _bdot(am_ref[...].astype(BF16), wm_ref[...])
    parts = []
    for h in range(A_HEADS):
        ol = ol_ref[:, h].reshape(m, KV_LORA).astype(BF16)
        parts.append(_bdot(ol, wuv_ref[h]))
    o = jnp.concatenate(parts, axis=-1).astype(BF16)
    y_a = _bdot(o, wo_ref[...])
    merged = (gates_ref[:, 0:D_MODEL] * y_conv + gates_ref[:, D_MODEL:2 * D_MODEL] * y_m
              + gates_ref[:, 2 * D_MODEL:3 * D_MODEL] * y_a)
    mixed = _bdot(merged.astype(BF16), wmix_ref[...])
    o_ref[...] = h_ref[...] + g2_ref[...] * mixed.reshape(g_sz, tt, D_MODEL)


def _merge_call(h, mods, layer, a_conv, a_m, o_lat, gates, wc, wm, wuv, wo, wmix, g_sz, tt):
    nseq, t, _ = h.shape
    m = g_sz * tt
    nj = t // tt
    rows = lambda w: pl.BlockSpec((m, w), lambda i, j: (i * nj + j, 0))
    row3 = pl.BlockSpec((g_sz, tt, D_MODEL), lambda i, j: (i, j, 0))
    wspec = lambda a, b: pl.BlockSpec((None, a, b), lambda i, j: (layer, 0, 0))
    return pl.pallas_call(
        _merge_kernel,
        out_shape=jax.ShapeDtypeStruct(h.shape, F32),
        grid=(nseq // g_sz, nj),
        in_specs=[
            row3,
            _mod_spec(layer, 5, g_sz),
            rows(CONV_WIDTH), rows(M_WIDTH),
            pl.BlockSpec((g_sz, A_HEADS, tt, KV_LORA), lambda i, j: (i, 0, j, 0)),
            rows(3 * D_MODEL),
            wspec(CONV_WIDTH, D_MODEL), wspec(M_WIDTH, D_MODEL),
            pl.BlockSpec((None, A_HEADS, KV_LORA, V_HEAD), lambda i, j: (layer, 0, 0, 0)),
            wspec(A_HEADS * V_HEAD, D_MODEL), wspec(D_MODEL, D_MODEL),
        ],
        out_specs=row3,
        compiler_params=_cparams(2),
        name="merge",
    )(h, mods, a_conv, a_m, o_lat, gates, wc, wm, wuv, wo, wmix)


def _rope_tables(pos):
    half = QK_ROPE // 2
    inv = ROPE_BASE ** (-jnp.arange(half, dtype=F32) * 2.0 / QK_ROPE)
    ang = pos.astype(F32)[:, None] * inv[None, :]
    cos, sin = jnp.cos(ang), jnp.sin(ang)
    n = pos.shape[0]
    ones = jnp.ones((n, SM_KR), F32)
    tail = LANES - SM_KR - QK_ROPE
    cos_t = jnp.concatenate([ones, cos, cos, jnp.ones((n, tail), F32)], axis=1)
    sin_t = jnp.concatenate([jnp.zeros((n, SM_KR), F32), -sin, sin, jnp.zeros((n, tail), F32)], axis=1)
    return cos_t, sin_t


def _rearranged_w_in(w_in):
    depth = w_in.shape[0]
    off = np.cumsum((0,) + (CONV_WIDTH,) * 3 + (M_WIDTH,) * 4 + (M_HEADS,) * 2
                    + (Q_LORA, KV_LORA, QK_ROPE) + (D_MODEL,) * 3)
    zi = w_in[:, :, off[7]:off[8]]
    zf = w_in[:, :, off[8]:off[9]]
    zdq = w_in[:, :, off[9]:off[10]]
    zdkv = w_in[:, :, off[10]:off[11]]
    zkr = w_in[:, :, off[11]:off[12]]
    gates = w_in[:, :, off[12]:off[15]]
    zeros = lambda n: jnp.zeros((depth, D_MODEL, n), w_in.dtype)
    half = QK_ROPE // 2
    small_a = jnp.concatenate([zi, zf, zeros(SM_KR - 2 * M_HEADS), zkr, zeros(LANES - SM_KR - QK_ROPE)], axis=2)
    small_b = jnp.concatenate([zeros(SM_KR), zkr[:, :, half:], zkr[:, :, :half],
                               zeros(LANES - SM_KR - QK_ROPE)], axis=2)
    return jnp.concatenate([w_in[:, :, 0:off[7]], zdq, zdkv, gates, small_a, small_b], axis=2).astype(BF16)


def _rearranged_w_uq(w_uq):
    depth = w_uq.shape[0]
    per = QK_NOPE + QK_ROPE
    half = QK_ROPE // 2
    w = w_uq.reshape(depth, Q_LORA, A_HEADS, per)
    nope = w[..., :QK_NOPE]
    r1 = w[..., QK_NOPE:QK_NOPE + half]
    r2 = w[..., QK_NOPE + half:]
    pad = jnp.zeros((depth, Q_LORA, A_HEADS, Q_SLOT - per), w_uq.dtype)
    wq = jnp.concatenate([nope, r1, r2, pad], axis=-1)
    wqs = jnp.concatenate([jnp.zeros_like(nope), r2, r1, pad], axis=-1)
    flat = lambda a: a.reshape(depth, Q_LORA, A_HEADS * Q_SLOT).astype(BF16)
    return flat(wq), flat(wqs)


def _block_diag_state(c):
    nseq = c.shape[0]
    eye = jnp.eye(M_HEADS, dtype=c.dtype)
    return jnp.einsum('bhde,hg->bhdge', c, eye).reshape(nseq, M_HEADS * M_DK, M_HEADS * M_DV)


def _diag_blocks(c_bd):
    nseq = c_bd.shape[0]
    c5 = c_bd.reshape(nseq, M_HEADS, M_DK, M_HEADS, M_DV)
    return jnp.stack([c5[:, h, :, h, :] for h in range(M_HEADS)], axis=1)


def _layer_group(x, mods, layer, wts, tiles, conv_prev, c0, n0, m0, cos_t, sin_t, attend, final):
    nseq, t, _ = x.shape
    g_sz, tt_ffn, tt_in, tt_q, tt_conv, tt_merge, lc, act_dtype, q_dtype = tiles
    h = _ffn_call(x, mods, layer, True, wts['ln_ffn1'], wts['ffn1_w13'], wts['ffn1_w2'], wts['final_norm'],
                  False, g_sz['ffn'], tt_ffn)
    za, cq, ckv, kcat, gates, small = _inproj_call(
        h, mods, layer, wts['ln_mix'], wts['w_in_r'], wts['mla_q_norm'], wts['mla_kv_norm'],
        cos_t, sin_t, wts['brow'], g_sz['in'], tt_in, act_dtype)
    za3 = za.reshape(nseq, t, ZA_W)
    small3 = small.reshape(nseq, t, LANES)
    a_conv, conv_new = _conv_call(za3, conv_prev, wts['conv_w'][layer], g_sz['conv'], tt_conv)
    gates_t = jnp.swapaxes(small3[:, :, 0:2 * M_HEADS], 1, 2)
    a_m, c_bd, n1, m1 = _mlstm_call(za3, small3, gates_t, c0, n0, m0, wts['mlstm_norm'][layer], lc, act_dtype)
    q = _qprep_call(cq, wts['wq'], wts['wqs'], wts['wuk'], cos_t, sin_t, layer, nseq, t, g_sz['q'], tt_q, q_dtype)
    o_lat = attend(q, kcat.reshape(nseq, t, QK_CAT))
    h2 = _merge_call(h, mods, layer, a_conv, a_m, o_lat, gates, wts['conv_out'], wts['mlstm_out'], wts['wuv'],
                     wts['mla_w_o'], wts['w_mix_out'], g_sz['merge'], tt_merge)
    out = _ffn_call(h2, mods, layer, False, wts['ln_ffn2'], wts['ffn2_w13'], wts['ffn2_w2'], wts['final_norm'],
                    final, g_sz['ffn'], tt_ffn)
    state = (ckv.reshape(nseq, t, KV_LORA), small3[:, :, SM_KR:SM_KR + QK_ROPE], conv_new,
             _diag_blocks(c_bd), n1.reshape(nseq, M_HEADS, M_DK), m1[:, 0, 0:M_HEADS])
    return out, state


def kernel(x_prompt, x_sample, cache_latent, cache_krope, state_conv, state_mlstm_C, state_mlstm_n, state_mlstm_m, page_table, c_prompt, c_sample, ln_ffn1, ffn1_w13, ffn1_w2, ln_mix, w_in, conv_w, conv_out, mlstm_b_i, mlstm_b_f, mlstm_norm, mlstm_out, mla_q_norm, mla_w_uq, mla_w_uk, mla_kv_norm, mla_w_uv, mla_w_o, w_mix_out, ln_ffn2, ffn2_w13, ffn2_w2, w_ada, b_ada, final_norm):
    bp, seq, _ = x_prompt.shape
    bs, dec, _ = x_sample.shape
    past_len = page_table.shape[1] * PAGE_SIZE

    c_all = jnp.concatenate([c_prompt, c_sample], axis=0)
    mods = _mods_call(c_all, w_ada, b_ada)
    mods_p = mods[:, :, :bp].reshape(DEPTH, 9, bp, 1, D_MODEL)
    mods_s = mods[:, :, bp:].reshape(DEPTH, 9, bs, 1, D_MODEL)

    wq, wqs = _rearranged_w_uq(mla_w_uq)
    wuk = mla_w_uk.reshape(DEPTH, KV_LORA, A_HEADS, QK_NOPE).transpose(0, 2, 3, 1)
    wuk = jnp.pad(wuk, ((0, 0), (0, 0), (0, Q_SLOT - QK_NOPE), (0, 0))).astype(BF16)
    brow = jnp.concatenate([mlstm_b_i, mlstm_b_f, jnp.zeros((DEPTH, LANES - 2 * M_HEADS), F32)], axis=1)
    wts = {
        'ln_ffn1': ln_ffn1.reshape(DEPTH, 1, D_MODEL), 'ffn1_w13': ffn1_w13.astype(BF16), 'ffn1_w2': ffn1_w2.astype(BF16),
        'ln_mix': ln_mix.reshape(DEPTH, 1, D_MODEL), 'w_in_r': _rearranged_w_in(w_in), 'conv_w': conv_w,
        'conv_out': conv_out.astype(BF16), 'brow': brow.reshape(DEPTH, 1, LANES),
        'mlstm_norm': mlstm_norm.reshape(DEPTH, 1, M_WIDTH), 'mlstm_out': mlstm_out.astype(BF16),
        'mla_q_norm': mla_q_norm.reshape(DEPTH, 1, Q_LORA), 'wq': wq, 'wqs': wqs, 'wuk': wuk, 'mla_kv_norm': mla_kv_norm.reshape(DEPTH, 1, KV_LORA),
        'wuv': mla_w_uv.reshape(DEPTH, KV_LORA, A_HEADS, V_HEAD).transpose(0, 2, 1, 3).astype(BF16),
        'mla_w_o': mla_w_o.astype(BF16), 'w_mix_out': w_mix_out.astype(BF16),
        'ln_ffn2': ln_ffn2.reshape(DEPTH, 1, D_MODEL), 'ffn2_w13': ffn2_w13.astype(BF16), 'ffn2_w2': ffn2_w2.astype(BF16),
        'final_norm': final_norm.reshape(1, D_MODEL),
    }

    cos_p, sin_p = _rope_tables(jnp.arange(seq))
    cos_s, sin_s = _rope_tables(past_len + jnp.arange(dec))
    cos_s = jnp.tile(cos_s, (bs, 1))
    sin_s = jnp.tile(sin_s, (bs, 1))

    one = lambda v: {'ffn': v, 'in': v, 'conv': v, 'q': v, 'merge': v}
    tiles_p = (one(1), 512, 256, 512, 512, 512, MLSTM_CHUNK_P, BF16, BF16)
    tiles_s = ({'ffn': 64, 'in': 32, 'conv': 128, 'q': 64, 'merge': 64}, dec, dec, dec, dec, dec, dec, F32, F32)

    zero_conv = jnp.zeros((bp, CONV_K - 1, CONV_WIDTH), F32)
    zero_c = jnp.zeros((bp, M_WIDTH, M_WIDTH), F32)
    zero_n = jnp.zeros((bp, 1, M_WIDTH), F32)
    zero_m = jnp.zeros((bp, 1, LANES), F32)

    xp, xs = x_prompt, x_sample
    p_st, s_st = [], []
    for layer in range(DEPTH):
        final = layer == DEPTH - 1
        xp, sp = _layer_group(xp, mods_p, layer, wts, tiles_p, zero_conv, zero_c, zero_n, zero_m,
                              cos_p, sin_p, _attn_prompt_call, final)
        p_st.append(sp)
        m0 = jnp.pad(state_mlstm_m[layer], ((0, 0), (0, LANES - M_HEADS))).reshape(bs, 1, LANES)
        attend_s = functools.partial(_attn_sample_call, page_table, cache_latent=cache_latent,
                                     cache_krope=cache_krope, layer=layer)
        xs, ss = _layer_group(xs, mods_s, layer, wts, tiles_s, state_conv[layer],
                              _block_diag_state(state_mlstm_C[layer]),
                              state_mlstm_n[layer].reshape(bs, 1, M_WIDTH), m0,
                              cos_s, sin_s, attend_s, final)
        s_st.append(ss)
    p_out = tuple(jnp.stack(a) for a in zip(*p_st))
    s_out = tuple(jnp.stack(a) for a in zip(*s_st))
    return (xp, xs) + p_out + s_out
```

```python
import functools

import jax
import jax.numpy as jnp
import numpy as np
from jax import lax
from jax.experimental import pallas as pl
from jax.experimental.pallas import tpu as pltpu

D_MODEL = 1024
DEPTH = 4
PAGE_SIZE = 128
D_FF = 2816
CONV_WIDTH = 256
CONV_K = 3
M_HEADS = 4
M_DK = 64
M_DV = 64
M_WIDTH = M_HEADS * M_DV
A_HEADS = 8
QK_NOPE = 64
QK_ROPE = 32
V_HEAD = 64
Q_LORA = 384
KV_LORA = 256
ROPE_BASE = 10000.0
EPS = 1e-6

LANES = 128
VMEM_LIMIT = 56 * 1024 * 1024

F32 = jnp.float32
BF16 = jnp.bfloat16
NEG_INF = float("-inf")
LOG2E = 1.4426950408889634

ZA_W = 7 * 256
OFF_DQ = ZA_W
OFF_DKV = OFF_DQ + Q_LORA
OFF_GATES = OFF_DKV + KV_LORA
OFF_SMALL_A = OFF_GATES + 3 * D_MODEL
OFF_SMALL_B = OFF_SMALL_A + LANES
N_IN_R = OFF_SMALL_B + LANES
SM_IG = 0
SM_LF = M_HEADS
SM_KR = 64
QK_CAT = KV_LORA + LANES
Q_SLOT = LANES

FFN_TF = 1408
ATT_TQ = 128
ATT_TK = 512
ATT_ROW_CHUNK = 32
MLSTM_CHUNK_P = 256


def _cparams(n_axes):
    return pltpu.CompilerParams(dimension_semantics=("arbitrary",) * n_axes,
                                vmem_limit_bytes=VMEM_LIMIT)


def _bdot(a, b):
    return jnp.dot(a, b, preferred_element_type=F32)


def _bdot_nt(a, b):
    return lax.dot_general(a, b, (((1,), (1,)), ((), ())), preferred_element_type=F32)


def _rms(x):
    return x * lax.rsqrt(jnp.mean(x * x, axis=-1, keepdims=True) + EPS)


def _mods_kernel(c_ref, w_ref, b_ref, o_ref):
    c = c_ref[...]
    a = (c * jax.nn.sigmoid(c)).astype(BF16)
    o_ref[...] = _bdot(a, w_ref[...].astype(BF16)) + b_ref[...]


def _mods_call(c_all, w_ada, b_ada):
    nb = c_all.shape[0]
    return pl.pallas_call(
        _mods_kernel,
        out_shape=jax.ShapeDtypeStruct((DEPTH, 9, nb, D_MODEL), F32),
        grid=(DEPTH, 9),
        in_specs=[
            pl.BlockSpec((nb, D_MODEL), lambda l, j: (0, 0)),
            pl.BlockSpec((None, D_MODEL, D_MODEL), lambda l, j: (l, 0, j)),
            pl.BlockSpec((None, 1, D_MODEL), lambda l, j: (l, 0, j)),
        ],
        out_specs=pl.BlockSpec((None, None, nb, D_MODEL), lambda l, j: (l, j, 0, 0)),
        compiler_params=_cparams(2),
        name="adaln_mods",
    )(c_all, w_ada, b_ada.reshape(DEPTH, 1, 9 * D_MODEL))


def _vec_spec(layer, width):
    return pl.BlockSpec((None, 1, width), lambda *_: (layer, 0, 0))


def _mod_spec(layer, which, g):
    return pl.BlockSpec((None, None, g, 1, D_MODEL), lambda i, *_: (layer, which, i, 0, 0))


def _ffn_kernel(x_ref, ln_ref, sh_ref, sc_ref, g_ref, w1_ref, w3_ref, w2_ref, fn_ref, o_ref,
                xn_ref, acc_ref, *, final):
    k = pl.program_id(2)
    g_sz, tt, _ = x_ref.shape

    @pl.when(k == 0)
    def _():
        y = _rms(x_ref[...]) * ln_ref[...]
        u = y * (1 + sc_ref[...]) + sh_ref[...]
        xn_ref[...] = u.reshape(g_sz * tt, D_MODEL).astype(BF16)
        acc_ref[...] = jnp.zeros_like(acc_ref)

    xn = xn_ref[...]
    gate = _bdot(xn, w1_ref[...])
    up = _bdot(xn, w3_ref[...])
    act = (gate * jax.nn.sigmoid(gate) * up).astype(BF16)
    acc_ref[...] += _bdot(act, w2_ref[...])

    @pl.when(k == pl.num_programs(2) - 1)
    def _():
        out = x_ref[...] + (0.5 * g_ref[...]) * acc_ref[...].reshape(g_sz, tt, D_MODEL)
        if final:
            out = _rms(out) * fn_ref[...]
        o_ref[...] = out


def _ffn_call(x, mods, layer, first, ln, w13, w2, fnorm, final, g_sz, tt):
    nseq, t, _ = x.shape
    nk = D_FF // FFN_TF
    base = 0 if first else 6
    grid = (nseq // g_sz, t // tt, nk)
    row = pl.BlockSpec((g_sz, tt, D_MODEL), lambda i, j, k: (i, j, 0))
    return pl.pallas_call(
        functools.partial(_ffn_kernel, final=final),
        out_shape=jax.ShapeDtypeStruct(x.shape, F32),
        grid=grid,
        in_specs=[
            row,
            _vec_spec(layer, D_MODEL),
            _mod_spec(layer, base + 0, g_sz),
            _mod_spec(layer, base + 1, g_sz),
            _mod_spec(layer, base + 2, g_sz),
            pl.BlockSpec((None, D_MODEL, FFN_TF), lambda i, j, k: (layer, 0, k)),
            pl.BlockSpec((None, D_MODEL, FFN_TF), lambda i, j, k: (layer, 0, k + nk)),
            pl.BlockSpec((None, FFN_TF, D_MODEL), lambda i, j, k: (layer, k, 0)),
            pl.BlockSpec((1, D_MODEL), lambda i, j, k: (0, 0)),
        ],
        out_specs=row,
        scratch_shapes=[pltpu.VMEM((g_sz * tt, D_MODEL), BF16),
                        pltpu.VMEM((g_sz * tt, D_MODEL), F32)],
        compiler_params=_cparams(3),
        name="ffn",
    )(x, ln, mods, mods, mods, w13, w13, w2, fnorm)


def _inproj_kernel(h_ref, ln_ref, sh_ref, sc_ref, w_ref, qn_ref, kvn_ref, cos_ref, sin_ref, brow_ref,
                   za_ref, cq_ref, ckv_ref, kcat_ref, gates_ref, small_ref):
    g_sz, tt, _ = h_ref.shape
    m = g_sz * tt
    y = _rms(h_ref[...]) * ln_ref[...]
    u = y * (1 + sc_ref[...]) + sh_ref[...]
    xn = u.reshape(m, D_MODEL).astype(BF16)

    za_ref[...] = _bdot(xn, w_ref[:, 0:ZA_W])
    zdq = _bdot(xn, w_ref[:, OFF_DQ:OFF_DKV])
    cq_ref[...] = (_rms(zdq) * qn_ref[...]).astype(cq_ref.dtype)
    zdkv = _bdot(xn, w_ref[:, OFF_DKV:OFF_GATES])
    ckv = _rms(zdkv) * kvn_ref[...]
    ckv_ref[...] = ckv
    gates_ref[...] = jax.nn.sigmoid(_bdot(xn, w_ref[:, OFF_GATES:OFF_SMALL_A])).astype(gates_ref.dtype)

    sa = _bdot(xn, w_ref[:, OFF_SMALL_A:OFF_SMALL_B])
    sb = _bdot(xn, w_ref[:, OFF_SMALL_B:N_IN_R])
    rot = sa * cos_ref[...] + sb * sin_ref[...]
    lane = lax.broadcasted_iota(jnp.int32, rot.shape, 1)
    zb = rot + brow_ref[...]
    logsig = jnp.minimum(zb, 0.0) - jnp.log1p(jnp.exp(-jnp.abs(zb)))
    is_kr = (lane >= SM_KR) & (lane < SM_KR + QK_ROPE)
    small = jnp.where(lane < SM_LF, zb,
                      jnp.where(lane < SM_LF + M_HEADS, logsig,
                                jnp.where(is_kr, rot, 0.0)))
    small_ref[...] = small
    kcat_ref[:, 0:KV_LORA] = ckv.astype(kcat_ref.dtype)
    kcat_ref[:, KV_LORA:QK_CAT] = jnp.where(is_kr, rot, 0.0).astype(kcat_ref.dtype)


def _inproj_call(h, mods, layer, ln, w_in_r, qn, kvn, cos_t, sin_t, brow, g_sz, tt, act_dtype):
    nseq, t, _ = h.shape
    n = nseq * t
    m = g_sz * tt
    nj = t // tt
    grid = (nseq // g_sz, nj)
    rows = lambda w: pl.BlockSpec((m, w), lambda i, j: (i * nj + j, 0))
    out_shapes = (
        jax.ShapeDtypeStruct((n, ZA_W), F32),
        jax.ShapeDtypeStruct((n, Q_LORA), BF16),
        jax.ShapeDtypeStruct((n, KV_LORA), F32),
        jax.ShapeDtypeStruct((n, QK_CAT), act_dtype),
        jax.ShapeDtypeStruct((n, 3 * D_MODEL), BF16),
        jax.ShapeDtypeStruct((n, LANES), F32),
    )
    return pl.pallas_call(
        _inproj_kernel,
        out_shape=out_shapes,
        grid=grid,
        in_specs=[
            pl.BlockSpec((g_sz, tt, D_MODEL), lambda i, j: (i, j, 0)),
            _vec_spec(layer, D_MODEL),
            _mod_spec(layer, 3, g_sz),
            _mod_spec(layer, 4, g_sz),
            pl.BlockSpec((None, D_MODEL, N_IN_R), lambda i, j: (layer, 0, 0)),
            _vec_spec(layer, Q_LORA),
            _vec_spec(layer, KV_LORA),
            pl.BlockSpec((m, LANES), lambda i, j: (j, 0)),
            pl.BlockSpec((m, LANES), lambda i, j: (j, 0)),
            _vec_spec(layer, LANES),
        ],
        out_specs=(rows(ZA_W), rows(Q_LORA), rows(KV_LORA), rows(QK_CAT), rows(3 * D_MODEL), rows(LANES)),
        compiler_params=_cparams(2),
        name="inproj",
    )(h, ln, mods, mods, w_in_r, qn, kvn, cos_t, sin_t, brow)


def _conv_kernel(zb_ref, zc_ref, zx_ref, prev_ref, w_ref, a_ref, st_ref, carry_ref):
    j = pl.program_id(1)
    g_sz, tt, c = zc_ref.shape

    @pl.when(j == 0)
    def _():
        carry_ref[...] = prev_ref[...]

    cu = zc_ref[...] * zx_ref[...]
    prev = carry_ref[...]
    t = lax.broadcasted_iota(jnp.int32, cu.shape, 1)
    p0 = prev[:, 0:1, :]
    p1 = prev[:, 1:2, :]
    c1 = jnp.where(t == 0, p1, pltpu.roll(cu, 1, axis=1))
    c2 = jnp.where(t == 0, p0, jnp.where(t == 1, p1, pltpu.roll(cu, 2, axis=1)))
    w = w_ref[...]
    conv = w[0:1, :] * c2 + w[1:2, :] * c1 + w[2:3, :] * cu
    a = zb_ref[...] * conv
    a_ref[...] = a.reshape(g_sz * tt, c).astype(a_ref.dtype)
    new = cu[:, tt - (CONV_K - 1):tt, :]
    carry_ref[...] = new
    st_ref[...] = new


def _conv_call(za3, prev, conv_w_l, g_sz, tt):
    nseq, t, _ = za3.shape
    nj = t // tt
    c = CONV_WIDTH
    col = lambda b: pl.BlockSpec((g_sz, tt, c), lambda i, j: (i, j, b))
    st = pl.BlockSpec((g_sz, CONV_K - 1, c), lambda i, j: (i, 0, 0))
    return pl.pallas_call(
        _conv_kernel,
        out_shape=(jax.ShapeDtypeStruct((nseq * t, c), BF16),
                   jax.ShapeDtypeStruct((nseq, CONV_K - 1, c), F32)),
        grid=(nseq // g_sz, nj),
        in_specs=[col(0), col(1), col(2), st,
                  pl.BlockSpec((CONV_K, c), lambda i, j: (0, 0))],
        out_specs=(pl.BlockSpec((g_sz * tt, c), lambda i, j: (i * nj + j, 0)), st),
        scratch_shapes=[pltpu.VMEM((g_sz, CONV_K - 1, c), F32)],
        compiler_params=_cparams(2),
        name="conv_mixer",
    )(za3, za3, za3, prev, conv_w_l)


def _mlstm_kernel(q_ref, k_ref, v_ref, zo_ref, sm_ref, gt_ref, c0_ref, n0_ref, m0_ref, nw_ref,
                  a_ref, c_ref, n_ref, m_ref, c_s, n_s, m_s):
    j = pl.program_id(1)
    lc = q_ref.shape[1]
    w = M_WIDTH

    @pl.when(j == 0)
    def _():
        c_s[...] = c0_ref[0]
        n_s[...] = n0_ref[0]
        m_s[...] = m0_ref[0]

    q = q_ref[0] * (M_DK ** -0.5)
    k = k_ref[0]
    v = v_ref[0]
    sm = sm_ref[0]
    gt = gt_ref[0]
    ri = lax.broadcasted_iota(jnp.int32, (lc, lc), 0)
    ci = lax.broadcasted_iota(jnp.int32, (lc, lc), 1)
    causal = ci <= ri
    bcol = jnp.dot(causal.astype(F32), sm, precision=lax.Precision.HIGHEST, preferred_element_type=F32)
    brow = jnp.dot(gt, (ri <= ci).astype(F32), precision=lax.Precision.HIGHEST, preferred_element_type=F32)

    lane = lax.broadcasted_iota(jnp.int32, (1, w), 1)
    lane_s = lax.broadcasted_iota(jnp.int32, (1, LANES), 1)
    cmat = c_s[...]
    nrow = n_s[...]
    mrow = m_s[...]
    kb = k.astype(BF16)
    vb = v.astype(BF16)
    qc = _bdot(q.astype(BF16), cmat.astype(BF16))

    zeros_l = jnp.zeros((lc, w), F32)
    num_intra = zeros_l
    den_l = zeros_l
    winter_l = zeros_l
    emt_l = zeros_l
    ws_l = zeros_l
    a_l = jnp.zeros((1, w), F32)
    m_new_row = mrow
    for h in range(M_HEADS):
        hm = (lane >= h * M_DV) & (lane < (h + 1) * M_DV)
        igc = sm[:, SM_IG + h:SM_IG + h + 1]
        bc = bcol[:, SM_LF + h:SM_LF + h + 1]
        igr = gt[SM_IG + h:SM_IG + h + 1, :]
        br = brow[SM_LF + h:SM_LF + h + 1, :]
        mp = mrow[:, h:h + 1]
        dlog = jnp.where(causal, bc - br + igr, NEG_INF)
        inter = bc + mp
        mt = jnp.maximum(inter, jnp.max(dlog, axis=1, keepdims=True))
        w_inter = jnp.exp(inter - mt)
        s = _bdot_nt(jnp.where(hm, q, 0.0).astype(BF16), kb) * jnp.exp(dlog - mt)
        num_intra = num_intra + _bdot(s.astype(BF16), jnp.where(hm, v, 0.0).astype(BF16))
        qn = jnp.sum(jnp.where(hm, q * nrow, 0.0), axis=1, keepdims=True)
        den = w_inter * qn + jnp.sum(s, axis=1, keepdims=True)
        den_l = jnp.where(hm, den, den_l)
        winter_l = jnp.where(hm, w_inter, winter_l)
        emt_l = jnp.where(hm, jnp.exp(-mt), emt_l)
        bl = bc[lc - 1:lc, :]
        glog = bl - bc + igc
        mn = jnp.maximum(bl + mp, jnp.max(glog, axis=0, keepdims=True))
        ws_l = jnp.where(hm, jnp.exp(glog - mn), ws_l)
        a_l = jnp.where(hm, jnp.exp(bl + mp - mn), a_l)
        m_new_row = jnp.where(lane_s == h, mn, m_new_row)

    num = winter_l * qc + num_intra
    hh = num / jnp.maximum(jnp.abs(den_l), emt_l)
    hsq = hh * hh
    ms_l = zeros_l
    for h in range(M_HEADS):
        hm = (lane >= h * M_DV) & (lane < (h + 1) * M_DV)
        ms = jnp.sum(jnp.where(hm, hsq, 0.0), axis=1, keepdims=True) * (1.0 / M_DV)
        ms_l = jnp.where(hm, ms, ms_l)
    hn = hh * lax.rsqrt(ms_l + EPS)
    a_ref[...] = (hn * nw_ref[...] * jax.nn.sigmoid(zo_ref[0])).astype(a_ref.dtype)

    kw = k * ws_l
    upd = lax.dot_general(kw.astype(BF16), vb, (((0,), (0,)), ((), ())), preferred_element_type=F32)
    r2 = lax.broadcasted_iota(jnp.int32, (w, w), 0)
    c2 = lax.broadcasted_iota(jnp.int32, (w, w), 1)
    same_head = (r2 // M_DK) == (c2 // M_DV)
    c_new = a_l * cmat + jnp.where(same_head, upd, 0.0)
    n_new = a_l * nrow + jnp.sum(kw, axis=0, keepdims=True)
    c_s[...] = c_new
    n_s[...] = n_new
    m_s[...] = m_new_row

    @pl.when(j == pl.num_programs(1) - 1)
    def _():
        c_ref[0] = c_new
        n_ref[0] = n_new
        m_ref[0] = m_new_row


def _mlstm_call(za3, small3, gates_t, c0, n0, m0, norm_w, lc, act_dtype):
    nseq, t, _ = za3.shape
    nj = t // lc
    w = M_WIDTH
    col = lambda b: pl.BlockSpec((1, lc, w), lambda i, j: (i, j, b))
    st_c = pl.BlockSpec((1, w, w), lambda i, j: (i, 0, 0))
    st_n = pl.BlockSpec((1, 1, w), lambda i, j: (i, 0, 0))
    st_m = pl.BlockSpec((1, 1, LANES), lambda i, j: (i, 0, 0))
    return pl.pallas_call(
        _mlstm_kernel,
        out_shape=(jax.ShapeDtypeStruct((nseq * t, w), act_dtype),
                   jax.ShapeDtypeStruct((nseq, w, w), F32),
                   jax.ShapeDtypeStruct((nseq, 1, w), F32),
                   jax.ShapeDtypeStruct((nseq, 1, LANES), F32)),
        grid=(nseq, nj),
        in_specs=[col(3), col(4), col(5), col(6),
                  pl.BlockSpec((1, lc, LANES), lambda i, j: (i, j, 0)),
                  pl.BlockSpec((1, 2 * M_HEADS, lc), lambda i, j: (i, 0, j)),
                  st_c, st_n, st_m,
                  pl.BlockSpec((1, w), lambda i, j: (0, 0))],
        out_specs=(pl.BlockSpec((lc, w), lambda i, j: (i * nj + j, 0)), st_c, st_n, st_m),
        scratch_shapes=[pltpu.VMEM((w, w), F32), pltpu.VMEM((1, w), F32), pltpu.VMEM((1, LANES), F32)],
        compiler_params=_cparams(2),
        name="mlstm",
    )(za3, za3, za3, za3, small3, gates_t, c0, n0, m0, norm_w)


def _qprep_kernel(cq_ref, wq_ref, wqs_ref, wuk_ref, cos_ref, sin_ref, q_ref):
    g_sz, _, tt, _ = q_ref.shape
    cq = cq_ref[...]
    qa = _bdot(cq, wq_ref[...])
    qs = _bdot(cq, wqs_ref[...])
    cos = cos_ref[...]
    sin = sin_ref[...]
    lane = lax.broadcasted_iota(jnp.int32, cos.shape, 1)
    is_rope = (lane >= SM_KR) & (lane < SM_KR + QK_ROPE)
    scale = (QK_NOPE + QK_ROPE) ** -0.5 * LOG2E
    for h in range(A_HEADS):
        slot = qa[:, h * Q_SLOT:(h + 1) * Q_SLOT] * cos + qs[:, h * Q_SLOT:(h + 1) * Q_SLOT] * sin
        q_lat = _bdot(slot.astype(BF16), wuk_ref[h]) * scale
        q_rope = jnp.where(is_rope, slot, 0.0) * scale
        q_ref[:, h, :, 0:KV_LORA] = q_lat.reshape(g_sz, tt, KV_LORA).astype(q_ref.dtype)
        q_ref[:, h, :, KV_LORA:QK_CAT] = q_rope.reshape(g_sz, tt, LANES).astype(q_ref.dtype)


def _qprep_call(cq, wq, wqs, wuk, cos_t, sin_t, layer, nseq, t, g_sz, tt, q_dtype):
    m = g_sz * tt
    nj = t // tt
    qw = A_HEADS * Q_SLOT
    return pl.pallas_call(
        _qprep_kernel,
        out_shape=jax.ShapeDtypeStruct((nseq, A_HEADS, t, QK_CAT), q_dtype),
        grid=(nseq // g_sz, nj),
        in_specs=[
            pl.BlockSpec((m, Q_LORA), lambda i, j: (i * nj + j, 0)),
            pl.BlockSpec((None, Q_LORA, qw), lambda i, j: (layer, 0, 0)),
            pl.BlockSpec((None, Q_LORA, qw), lambda i, j: (layer, 0, 0)),
            pl.BlockSpec((None, A_HEADS, Q_SLOT, KV_LORA), lambda i, j: (layer, 0, 0, 0)),
            pl.BlockSpec((m, LANES), lambda i, j: (j, 0)),
            pl.BlockSpec((m, LANES), lambda i, j: (j, 0)),
        ],
        out_specs=pl.BlockSpec((g_sz, A_HEADS, tt, QK_CAT), lambda i, j: (i, 0, j, 0)),
        compiler_params=_cparams(2),
        name="mla_qprep",
    )(cq, wq, wqs, wuk, cos_t, sin_t)


def _fold_lanes(x, op):
    out = x[:, 0:LANES]
    for j in range(1, x.shape[1] // LANES):
        out = op(out, x[:, j * LANES:(j + 1) * LANES])
    return out


def _attn_prompt_kernel(q_ref, k_ref, o_ref, s_ref, p_ref, m_ref, al_ref, l_ref, acc_ref, *, tk):
    qi = pl.program_id(1)
    tq = q_ref.shape[2]
    rows = A_HEADS * tq
    half = rows // 2
    hh = A_HEADS // 2
    rc = ATT_ROW_CHUNK
    n_lane_groups = tk // LANES
    m_ref[...] = jnp.full_like(m_ref, NEG_INF)
    l_ref[...] = jnp.zeros_like(l_ref)
    acc_ref[...] = jnp.zeros_like(acc_ref)
    col_minus_row = (lax.broadcasted_iota(jnp.int32, (rc, tk), 1)
                     - lax.broadcasted_iota(jnp.int32, (rc, tk), 0))

    def block(kb, masked):
        k = k_ref[pl.ds(pl.multiple_of(kb * tk, tk), tk), :]
        for b in range(2):
            qh = q_ref[0, b * hh:(b + 1) * hh].reshape(half, QK_CAT)
            s_ref[b * half:(b + 1) * half, :] = _bdot_nt(qh, k)
        for c in range(rows // rc):
            r = slice(c * rc, (c + 1) * rc)
            sc = s_ref[r, :]
            if masked:
                thr = qi * tq - kb * tk + (c * rc) % tq
                sc = jnp.where(col_minus_row <= thr, sc, NEG_INF)
                s_ref[r, :] = sc
            m_prev = m_ref[r, :]
            m_new = jnp.maximum(m_prev, jnp.max(_fold_lanes(sc, jnp.maximum), axis=1, keepdims=True))
            al_ref[r, :] = jnp.exp2(m_prev - m_new)
            m_ref[r, :] = m_new
        for c in range(rows // rc):
            r = slice(c * rc, (c + 1) * rc)
            m_new = m_ref[r, :]
            part = None
            for j in range(n_lane_groups):
                cols = slice(j * LANES, (j + 1) * LANES)
                p = jnp.exp2(s_ref[r, cols] - m_new)
                p_ref[r, cols] = p.astype(BF16)
                part = p if part is None else part + p
            l_ref[r, :] = al_ref[r, :] * l_ref[r, :] + part
        kv = k[:, 0:KV_LORA]
        for b in range(2):
            r = slice(b * half, (b + 1) * half)
            pv = _bdot(p_ref[r, :], kv)
            alpha = al_ref[r, :]
            for j in range(KV_LORA // LANES):
                cols = slice(j * LANES, (j + 1) * LANES)
                acc_ref[r, cols] = alpha * acc_ref[r, cols] + pv[:, cols]

    n_kb = (qi * tq + tq + tk - 1) // tk

    def body(kb, carry):
        block(kb, False)
        return carry

    lax.fori_loop(0, n_kb - 1, body, 0)
    block(n_kb - 1, True)
    l_row = jnp.sum(l_ref[...], axis=1, keepdims=True)
    o_ref[0] = (acc_ref[...] / l_row).reshape(A_HEADS, tq, KV_LORA).astype(o_ref.dtype)


def _attn_prompt_call(q, kcat3):
    nseq, _, t, _ = q.shape
    tq = min(ATT_TQ, t)
    tk = min(ATT_TK, t)
    assert tk % tq == 0 and tq % ATT_ROW_CHUNK == 0 and t % tk == 0
    rows = A_HEADS * tq
    return pl.pallas_call(
        functools.partial(_attn_prompt_kernel, tk=tk),
        out_shape=jax.ShapeDtypeStruct((nseq, A_HEADS, t, KV_LORA), BF16),
        grid=(nseq, t // tq),
        in_specs=[pl.BlockSpec((1, A_HEADS, tq, QK_CAT), lambda i, j: (i, 0, j, 0)),
                  pl.BlockSpec((None, t, QK_CAT), lambda i, j: (i, 0, 0))],
        out_specs=pl.BlockSpec((1, A_HEADS, tq, KV_LORA), lambda i, j: (i, 0, j, 0)),
        scratch_shapes=[pltpu.VMEM((rows, tk), F32), pltpu.VMEM((rows, tk), BF16),
                        pltpu.VMEM((rows, LANES), F32), pltpu.VMEM((rows, LANES), F32),
                        pltpu.VMEM((rows, LANES), F32), pltpu.VMEM((rows, KV_LORA), F32)],
        compiler_params=_cparams(2),
        name="attn_prompt",
    )(q, kcat3)


def _attn_sample_kernel(pt_ref, q_ref, knew_ref, *refs, n_pages):
    del pt_ref
    lat_refs = refs[0:n_pages]
    krt_refs = refs[n_pages:2 * n_pages]
    o_ref, kbuf, krbuf = refs[2 * n_pages:]
    dec = q_ref.shape[2]
    rows = A_HEADS * dec
    for i in range(n_pages):
        kbuf[i * PAGE_SIZE:(i + 1) * PAGE_SIZE, :] = lat_refs[i][...].astype(BF16)
        krbuf[:, i * PAGE_SIZE:(i + 1) * PAGE_SIZE] = krt_refs[i][...].astype(BF16)
    q = q_ref[0].reshape(rows, QK_CAT)
    q_lat = q[:, 0:KV_LORA].astype(BF16)
    q_rope = q[:, KV_LORA + SM_KR:KV_LORA + SM_KR + QK_ROPE].astype(BF16)
    kv = kbuf[...]
    s = _bdot_nt(q_lat, kv) + _bdot(q_rope, krbuf[...])
    knew = knew_ref[0].astype(BF16)
    s2 = _bdot_nt(q.astype(BF16), knew)
    row = lax.broadcasted_iota(jnp.int32, s2.shape, 0)
    col = lax.broadcasted_iota(jnp.int32, s2.shape, 1)
    s2 = jnp.where(col <= (row & (dec - 1)), s2, NEG_INF)
    m = jnp.maximum(jnp.max(s, axis=1, keepdims=True), jnp.max(s2, axis=1, keepdims=True))
    p = jnp.exp2(s - m)
    p2 = jnp.exp2(s2 - m)
    l = jnp.sum(p, axis=1, keepdims=True) + jnp.sum(p2, axis=1, keepdims=True)
    o = _bdot(p.astype(BF16), kv) + _bdot(p2.astype(BF16), knew[:, 0:KV_LORA])
    o_ref[0] = (o / l).reshape(A_HEADS, dec, KV_LORA)


def _attn_sample_call(page_table, q, kcat3, cache_latent, cache_krope_t, layer):
    nseq, _, dec, _ = q.shape
    n_pages = page_table.shape[1]
    assert dec & (dec - 1) == 0

    def page_spec(i, shape):
        return pl.BlockSpec((None, None) + shape, lambda s, pt: (layer, pt[s, i], 0, 0))

    in_specs = [pl.BlockSpec((1, A_HEADS, dec, QK_CAT), lambda s, pt: (s, 0, 0, 0)),
                pl.BlockSpec((1, dec, QK_CAT), lambda s, pt: (s, 0, 0))]
    in_specs += [page_spec(i, (PAGE_SIZE, KV_LORA)) for i in range(n_pages)]
    in_specs += [page_spec(i, (QK_ROPE, PAGE_SIZE)) for i in range(n_pages)]
    grid_spec = pltpu.PrefetchScalarGridSpec(
        num_scalar_prefetch=1,
        grid=(nseq,),
        in_specs=in_specs,
        out_specs=pl.BlockSpec((1, A_HEADS, dec, KV_LORA), lambda s, pt: (s, 0, 0, 0)),
        scratch_shapes=[pltpu.VMEM((n_pages * PAGE_SIZE, KV_LORA), BF16),
                        pltpu.VMEM((QK_ROPE, n_pages * PAGE_SIZE), BF16)],
    )
    return pl.pallas_call(
        functools.partial(_attn_sample_kernel, n_pages=n_pages),
        out_shape=jax.ShapeDtypeStruct((nseq, A_HEADS, dec, KV_LORA), F32),
        grid_spec=grid_spec,
        compiler_params=_cparams(1),
        name="attn_sample",
    )(page_table, q, kcat3, *([cache_latent] * n_pages), *([cache_krope_t] * n_pages))


def _merge_kernel(h_ref, g2_ref, ac_ref, am_ref, ol_ref, gates_ref, wc_ref, wm_ref, wuv_ref, wo_ref, wmix_ref,
                  o_ref):
    g_sz, tt, _ = h_ref.shape
    m = g_sz * tt
    y_conv = _bdot(ac_ref[...].astype(BF16), wc_ref[...])
    y_m = _bdot(am_ref[...].astype(BF16), wm_ref[...])
    parts = []
    for h in range(A_HEADS):
        ol = ol_ref[:, h].reshape(m, KV_LORA).astype(BF16)
        parts.append(_bdot(ol, wuv_ref[h]))
    o = jnp.concatenate(parts, axis=-1).astype(BF16)
    y_a = _bdot(o, wo_ref[...])
    gate = lambda b: gates_ref[:, b * D_MODEL:(b + 1) * D_MODEL].astype(F32)
    merged = gate(0) * y_conv + gate(1) * y_m + gate(2) * y_a
    mixed = _bdot(merged.astype(BF16), wmix_ref[...])
    o_ref[...] = h_ref[...] + g2_ref[...] * mixed.reshape(g_sz, tt, D_MODEL)


def _merge_call(h, mods, layer, a_conv, a_m, o_lat, gates, wc, wm, wuv, wo, wmix, g_sz, tt):
    nseq, t, _ = h.shape
    m = g_sz * tt
    nj = t // tt
    rows = lambda w: pl.BlockSpec((m, w), lambda i, j: (i * nj + j, 0))
    row3 = pl.BlockSpec((g_sz, tt, D_MODEL), lambda i, j: (i, j, 0))
    wspec = lambda a, b: pl.BlockSpec((None, a, b), lambda i, j: (layer, 0, 0))
    return pl.pallas_call(
        _merge_kernel,
        out_shape=jax.ShapeDtypeStruct(h.shape, F32),
        grid=(nseq // g_sz, nj),
        in_specs=[
            row3,
            _mod_spec(layer, 5, g_sz),
            rows(CONV_WIDTH), rows(M_WIDTH),
            pl.BlockSpec((g_sz, A_HEADS, tt, KV_LORA), lambda i, j: (i, 0, j, 0)),
            rows(3 * D_MODEL),
            wspec(CONV_WIDTH, D_MODEL), wspec(M_WIDTH, D_MODEL),
            pl.BlockSpec((None, A_HEADS, KV_LORA, V_HEAD), lambda i, j: (layer, 0, 0, 0)),
            wspec(A_HEADS * V_HEAD, D_MODEL), wspec(D_MODEL, D_MODEL),
        ],
        out_specs=row3,
        compiler_params=_cparams(2),
        name="merge",
    )(h, mods, a_conv, a_m, o_lat, gates, wc, wm, wuv, wo, wmix)


def _rope_tables(pos):
    half = QK_ROPE // 2
    inv = ROPE_BASE ** (-jnp.arange(half, dtype=F32) * 2.0 / QK_ROPE)
    ang = pos.astype(F32)[:, None] * inv[None, :]
    cos, sin = jnp.cos(ang), jnp.sin(ang)
    n = pos.shape[0]
    ones = jnp.ones((n, SM_KR), F32)
    tail = LANES - SM_KR - QK_ROPE
    cos_t = jnp.concatenate([ones, cos, cos, jnp.ones((n, tail), F32)], axis=1)
    sin_t = jnp.concatenate([jnp.zeros((n, SM_KR), F32), -sin, sin, jnp.zeros((n, tail), F32)], axis=1)
    return cos_t, sin_t


def _rearranged_w_in(w_in):
    depth = w_in.shape[0]
    off = np.cumsum((0,) + (CONV_WIDTH,) * 3 + (M_WIDTH,) * 4 + (M_HEADS,) * 2
                    + (Q_LORA, KV_LORA, QK_ROPE) + (D_MODEL,) * 3)
    zi = w_in[:, :, off[7]:off[8]]
    zf = w_in[:, :, off[8]:off[9]]
    zdq = w_in[:, :, off[9]:off[10]]
    zdkv = w_in[:, :, off[10]:off[11]]
    zkr = w_in[:, :, off[11]:off[12]]
    gates = w_in[:, :, off[12]:off[15]]
    zeros = lambda n: jnp.zeros((depth, D_MODEL, n), w_in.dtype)
    half = QK_ROPE // 2
    small_a = jnp.concatenate([zi, zf, zeros(SM_KR - 2 * M_HEADS), zkr, zeros(LANES - SM_KR - QK_ROPE)], axis=2)
    small_b = jnp.concatenate([zeros(SM_KR), zkr[:, :, half:], zkr[:, :, :half],
                               zeros(LANES - SM_KR - QK_ROPE)], axis=2)
    return jnp.concatenate([w_in[:, :, 0:off[7]], zdq, zdkv, gates, small_a, small_b], axis=2).astype(BF16)


def _rearranged_w_uq(w_uq):
    depth = w_uq.shape[0]
    per = QK_NOPE + QK_ROPE
    half = QK_ROPE // 2
    w = w_uq.reshape(depth, Q_LORA, A_HEADS, per)
    nope = w[..., :QK_NOPE]
    r1 = w[..., QK_NOPE:QK_NOPE + half]
    r2 = w[..., QK_NOPE + half:]
    pad = jnp.zeros((depth, Q_LORA, A_HEADS, Q_SLOT - per), w_uq.dtype)
    wq = jnp.concatenate([nope, r1, r2, pad], axis=-1)
    wqs = jnp.concatenate([jnp.zeros_like(nope), r2, r1, pad], axis=-1)
    flat = lambda a: a.reshape(depth, Q_LORA, A_HEADS * Q_SLOT).astype(BF16)
    return flat(wq), flat(wqs)


def _block_diag_state(c):
    nseq = c.shape[0]
    eye = jnp.eye(M_HEADS, dtype=c.dtype)
    return jnp.einsum('bhde,hg->bhdge', c, eye).reshape(nseq, M_HEADS * M_DK, M_HEADS * M_DV)


def _diag_blocks(c_bd):
    nseq = c_bd.shape[0]
    c5 = c_bd.reshape(nseq, M_HEADS, M_DK, M_HEADS, M_DV)
    return jnp.stack([c5[:, h, :, h, :] for h in range(M_HEADS)], axis=1)


def _tile_plan(nseq, t):
    if t >= MLSTM_CHUNK_P:
        blk = lambda rows: (1, min(rows, t))
        act = BF16
    else:
        blk = lambda rows: (min(max(rows // t, 1), nseq), t)
        act = F32
    plan = dict(ffn=blk(512), inproj=blk(256), conv=blk(1024 if act == F32 else 512), qprep=blk(512),
                merge=blk(512), chunk=min(MLSTM_CHUNK_P, t), act=act)
    for name in ('ffn', 'inproj', 'conv', 'qprep', 'merge'):
        g_sz, tt = plan[name]
        assert nseq % g_sz == 0 and t % tt == 0, (name, nseq, t)
    return plan


def _layer_group(x, mods, layer, wts, conv_prev, c0, n0, m0, cos_t, sin_t, attend, final):
    nseq, t, _ = x.shape
    plan = _tile_plan(nseq, t)
    act_dtype = plan['act']
    h = _ffn_call(x, mods, layer, True, wts['ln_ffn1'], wts['ffn1_w13'], wts['ffn1_w2'], wts['final_norm'],
                  False, *plan['ffn'])
    za, cq, ckv, kcat, gates, small = _inproj_call(
        h, mods, layer, wts['ln_mix'], wts['w_in_r'], wts['mla_q_norm'], wts['mla_kv_norm'],
        cos_t, sin_t, wts['brow'], *plan['inproj'], act_dtype)
    za3 = za.reshape(nseq, t, ZA_W)
    small3 = small.reshape(nseq, t, LANES)
    a_conv, conv_new = _conv_call(za3, conv_prev, wts['conv_w'][layer], *plan['conv'])
    gates_t = jnp.swapaxes(small3[:, :, 0:2 * M_HEADS], 1, 2)
    a_m, c_bd, n1, m1 = _mlstm_call(za3, small3, gates_t, c0, n0, m0, wts['mlstm_norm'][layer], plan['chunk'],
                                    act_dtype)
    q = _qprep_call(cq, wts['wq'], wts['wqs'], wts['wuk'], cos_t, sin_t, layer, nseq, t, *plan['qprep'], act_dtype)
    o_lat = attend(q, kcat.reshape(nseq, t, QK_CAT))
    h2 = _merge_call(h, mods, layer, a_conv, a_m, o_lat, gates, wts['conv_out'], wts['mlstm_out'], wts['wuv'],
                     wts['mla_w_o'], wts['w_mix_out'], *plan['merge'])
    out = _ffn_call(h2, mods, layer, False, wts['ln_ffn2'], wts['ffn2_w13'], wts['ffn2_w2'], wts['final_norm'],
                    final, *plan['ffn'])
    state = (ckv.reshape(nseq, t, KV_LORA), small3[:, :, SM_KR:SM_KR + QK_ROPE], conv_new,
             _diag_blocks(c_bd), n1.reshape(nseq, M_HEADS, M_DK), m1[:, 0, 0:M_HEADS])
    return out, state


def kernel(x_prompt, x_sample, cache_latent, cache_krope, state_conv, state_mlstm_C, state_mlstm_n, state_mlstm_m, page_table, c_prompt, c_sample, ln_ffn1, ffn1_w13, ffn1_w2, ln_mix, w_in, conv_w, conv_out, mlstm_b_i, mlstm_b_f, mlstm_norm, mlstm_out, mla_q_norm, mla_w_uq, mla_w_uk, mla_kv_norm, mla_w_uv, mla_w_o, w_mix_out, ln_ffn2, ffn2_w13, ffn2_w2, w_ada, b_ada, final_norm):
    bp, seq, _ = x_prompt.shape
    bs, dec, _ = x_sample.shape
    past_len = page_table.shape[1] * PAGE_SIZE

    c_all = jnp.concatenate([c_prompt, c_sample], axis=0)
    mods = _mods_call(c_all, w_ada, b_ada)
    mods_p = mods[:, :, :bp].reshape(DEPTH, 9, bp, 1, D_MODEL)
    mods_s = mods[:, :, bp:].reshape(DEPTH, 9, bs, 1, D_MODEL)

    wq, wqs = _rearranged_w_uq(mla_w_uq)
    wuk = mla_w_uk.reshape(DEPTH, KV_LORA, A_HEADS, QK_NOPE).transpose(0, 2, 3, 1)
    wuk = jnp.pad(wuk, ((0, 0), (0, 0), (0, Q_SLOT - QK_NOPE), (0, 0))).astype(BF16)
    brow = jnp.concatenate([mlstm_b_i, mlstm_b_f, jnp.zeros((DEPTH, LANES - 2 * M_HEADS), F32)], axis=1)
    wts = {
        'ln_ffn1': ln_ffn1.reshape(DEPTH, 1, D_MODEL), 'ffn1_w13': ffn1_w13.astype(BF16), 'ffn1_w2': ffn1_w2.astype(BF16),
        'ln_mix': ln_mix.reshape(DEPTH, 1, D_MODEL), 'w_in_r': _rearranged_w_in(w_in), 'conv_w': conv_w,
        'conv_out': conv_out.astype(BF16), 'brow': brow.reshape(DEPTH, 1, LANES),
        'mlstm_norm': mlstm_norm.reshape(DEPTH, 1, M_WIDTH), 'mlstm_out': mlstm_out.astype(BF16),
        'mla_q_norm': mla_q_norm.reshape(DEPTH, 1, Q_LORA), 'wq': wq, 'wqs': wqs, 'wuk': wuk, 'mla_kv_norm': mla_kv_norm.reshape(DEPTH, 1, KV_LORA),
        'wuv': mla_w_uv.reshape(DEPTH, KV_LORA, A_HEADS, V_HEAD).transpose(0, 2, 1, 3).astype(BF16),
        'mla_w_o': mla_w_o.astype(BF16), 'w_mix_out': w_mix_out.astype(BF16),
        'ln_ffn2': ln_ffn2.reshape(DEPTH, 1, D_MODEL), 'ffn2_w13': ffn2_w13.astype(BF16), 'ffn2_w2': ffn2_w2.astype(BF16),
        'final_norm': final_norm.reshape(1, D_MODEL),
    }

    cos_p, sin_p = _rope_tables(jnp.arange(seq))
    cos_s, sin_s = _rope_tables(past_len + jnp.arange(dec))
    cos_s = jnp.tile(cos_s, (bs, 1))
    sin_s = jnp.tile(sin_s, (bs, 1))

    cache_krope_t = jnp.swapaxes(cache_krope, 2, 3)

    zero_conv = jnp.zeros((bp, CONV_K - 1, CONV_WIDTH), F32)
    zero_c = jnp.zeros((bp, M_WIDTH, M_WIDTH), F32)
    zero_n = jnp.zeros((bp, 1, M_WIDTH), F32)
    zero_m = jnp.zeros((bp, 1, LANES), F32)

    xp, xs = x_prompt, x_sample
    p_st, s_st = [], []
    for layer in range(DEPTH):
        final = layer == DEPTH - 1
        xp, sp = _layer_group(xp, mods_p, layer, wts, zero_conv, zero_c, zero_n, zero_m,
                              cos_p, sin_p, _attn_prompt_call, final)
        p_st.append(sp)
        m0 = jnp.pad(state_mlstm_m[layer], ((0, 0), (0, LANES - M_HEADS))).reshape(bs, 1, LANES)
        attend_s = functools.partial(_attn_sample_call, page_table, cache_latent=cache_latent,
                                     cache_krope_t=cache_krope_t, layer=layer)
        xs, ss = _layer_group(xs, mods_s, layer, wts, state_conv[layer],
                              _block_diag_state(state_mlstm_C[layer]),
                              state_mlstm_n[layer].reshape(bs, 1, M_WIDTH), m0,
                              cos_s, sin_s, attend_s, final)
        s_st.append(ss)
    p_out = tuple(jnp.stack(a) for a in zip(*p_st))
    s_out = tuple(jnp.stack(a) for a in zip(*s_st))
    return (xp, xs) + p_out + s_out
```

```python
import functools

import jax
import jax.numpy as jnp
import numpy as np
from jax import lax
from jax.experimental import pallas as pl
from jax.experimental.pallas import tpu as pltpu

D_MODEL = 1024
DEPTH = 4
PAGE_SIZE = 128
D_FF = 2816
CONV_WIDTH = 256
CONV_K = 3
M_HEADS = 4
M_DK = 64
M_DV = 64
M_WIDTH = M_HEADS * M_DV
A_HEADS = 8
QK_NOPE = 64
QK_ROPE = 32
V_HEAD = 64
Q_LORA = 384
KV_LORA = 256
ROPE_BASE = 10000.0
EPS = 1e-6

LANES = 128
VMEM_LIMIT = 56 * 1024 * 1024

F32 = jnp.float32
BF16 = jnp.bfloat16
NEG_INF = float("-inf")
LOG2E = 1.4426950408889634

ZA_W = 7 * 256
OFF_DQ = ZA_W
OFF_DKV = OFF_DQ + Q_LORA
OFF_GATES = OFF_DKV + KV_LORA
OFF_SMALL_A = OFF_GATES + 3 * D_MODEL
OFF_SMALL_B = OFF_SMALL_A + LANES
N_IN_R = OFF_SMALL_B + LANES
SM_IG = 0
SM_LF = M_HEADS
SM_KR = 64
QK_CAT = KV_LORA + LANES
Q_SLOT = LANES

FFN_TF = 1408
ATT_TQ = 128
ATT_TK = 512
ATT_ROW_CHUNK = 32
MLSTM_CHUNK_P = 256
MLSTM_SEQS_P = 1
MLSTM_SEQS_S = 8


def _cparams(n_axes):
    return pltpu.CompilerParams(dimension_semantics=("arbitrary",) * n_axes,
                                vmem_limit_bytes=VMEM_LIMIT)


def _bdot(a, b):
    return jnp.dot(a, b, preferred_element_type=F32)


def _bdot_nt(a, b):
    return lax.dot_general(a, b, (((1,), (1,)), ((), ())), preferred_element_type=F32)


def _rms(x):
    return x * lax.rsqrt(jnp.mean(x * x, axis=-1, keepdims=True) + EPS)


def _mods_kernel(c_ref, w_ref, b_ref, o_ref):
    c = c_ref[...]
    a = (c * jax.nn.sigmoid(c)).astype(BF16)
    o_ref[...] = _bdot(a, w_ref[...].astype(BF16)) + b_ref[...]


def _mods_call(c_all, w_ada, b_ada):
    nb = c_all.shape[0]
    return pl.pallas_call(
        _mods_kernel,
        out_shape=jax.ShapeDtypeStruct((DEPTH, 9, nb, D_MODEL), F32),
        grid=(DEPTH, 9),
        in_specs=[
            pl.BlockSpec((nb, D_MODEL), lambda l, j: (0, 0)),
            pl.BlockSpec((None, D_MODEL, D_MODEL), lambda l, j: (l, 0, j)),
            pl.BlockSpec((None, 1, D_MODEL), lambda l, j: (l, 0, j)),
        ],
        out_specs=pl.BlockSpec((None, None, nb, D_MODEL), lambda l, j: (l, j, 0, 0)),
        compiler_params=_cparams(2),
        name="adaln_mods",
    )(c_all, w_ada, b_ada.reshape(DEPTH, 1, 9 * D_MODEL))


def _vec_spec(layer, width):
    return pl.BlockSpec((None, 1, width), lambda *_: (layer, 0, 0))


def _mod_spec(layer, which, g):
    return pl.BlockSpec((None, None, g, 1, D_MODEL), lambda i, *_: (layer, which, i, 0, 0))


def _ffn_kernel(x_ref, ln_ref, sh_ref, sc_ref, g_ref, w13_ref, w2_ref, fn_ref, o_ref, *, final):
    g_sz, tt, _ = x_ref.shape
    x = x_ref[...]
    u = (_rms(x) * ln_ref[...]) * (1 + sc_ref[...]) + sh_ref[...]
    xn = u.reshape(g_sz * tt, D_MODEL).astype(BF16)
    y = None
    for c in range(D_FF // FFN_TF):
        cols = slice(c * FFN_TF, (c + 1) * FFN_TF)
        gate = _bdot(xn, w13_ref[:, cols])
        up = _bdot(xn, w13_ref[:, D_FF + c * FFN_TF:D_FF + (c + 1) * FFN_TF])
        act = (gate * jax.nn.sigmoid(gate) * up).astype(BF16)
        part = _bdot(act, w2_ref[cols, :])
        y = part if y is None else y + part
    out = x + (0.5 * g_ref[...]) * y.reshape(g_sz, tt, D_MODEL)
    if final:
        out = _rms(out) * fn_ref[...]
    o_ref[...] = out


def _ffn_call(x, mods, layer, first, ln, w13, w2, fnorm, final, g_sz, tt):
    nseq, t, _ = x.shape
    base = 0 if first else 6
    row = pl.BlockSpec((g_sz, tt, D_MODEL), lambda i, j: (i, j, 0))
    resident = lambda a, b: pl.BlockSpec((None, a, b), lambda i, j: (layer, 0, 0), pipeline_mode=pl.Buffered(1))
    return pl.pallas_call(
        functools.partial(_ffn_kernel, final=final),
        out_shape=jax.ShapeDtypeStruct(x.shape, F32),
        grid=(nseq // g_sz, t // tt),
        in_specs=[
            row,
            _vec_spec(layer, D_MODEL),
            _mod_spec(layer, base + 0, g_sz),
            _mod_spec(layer, base + 1, g_sz),
            _mod_spec(layer, base + 2, g_sz),
            resident(D_MODEL, 2 * D_FF),
            resident(D_FF, D_MODEL),
            pl.BlockSpec((1, D_MODEL), lambda i, j: (0, 0)),
        ],
        out_specs=row,
        compiler_params=_cparams(2),
        name="ffn",
    )(x, ln, mods, mods, mods, w13, w2, fnorm)


def _inproj_kernel(h_ref, ln_ref, sh_ref, sc_ref, w_ref, qn_ref, kvn_ref, cos_ref, sin_ref, brow_ref,
                   za_ref, cq_ref, ckv_ref, kcat_ref, gates_ref, small_ref):
    g_sz, tt, _ = h_ref.shape
    m = g_sz * tt
    y = _rms(h_ref[...]) * ln_ref[...]
    u = y * (1 + sc_ref[...]) + sh_ref[...]
    xn = u.reshape(m, D_MODEL).astype(BF16)

    za_ref[...] = _bdot(xn, w_ref[:, 0:ZA_W])
    zdq = _bdot(xn, w_ref[:, OFF_DQ:OFF_DKV])
    cq_ref[...] = (_rms(zdq) * qn_ref[...]).astype(cq_ref.dtype)
    zdkv = _bdot(xn, w_ref[:, OFF_DKV:OFF_GATES])
    ckv = _rms(zdkv) * kvn_ref[...]
    ckv_ref[...] = ckv
    gates_ref[...] = jax.nn.sigmoid(_bdot(xn, w_ref[:, OFF_GATES:OFF_SMALL_A])).astype(gates_ref.dtype)

    sa = _bdot(xn, w_ref[:, OFF_SMALL_A:OFF_SMALL_B])
    sb = _bdot(xn, w_ref[:, OFF_SMALL_B:N_IN_R])
    rot = sa * cos_ref[...] + sb * sin_ref[...]
    lane = lax.broadcasted_iota(jnp.int32, rot.shape, 1)
    zb = rot + brow_ref[...]
    logsig = jnp.minimum(zb, 0.0) - jnp.log1p(jnp.exp(-jnp.abs(zb)))
    is_kr = (lane >= SM_KR) & (lane < SM_KR + QK_ROPE)
    small = jnp.where(lane < SM_LF, zb,
                      jnp.where(lane < SM_LF + M_HEADS, logsig,
                                jnp.where(is_kr, rot, 0.0)))
    small_ref[...] = small
    kcat_ref[:, 0:KV_LORA] = ckv.astype(kcat_ref.dtype)
    kcat_ref[:, KV_LORA:QK_CAT] = jnp.where(is_kr, rot, 0.0).astype(kcat_ref.dtype)


def _inproj_call(h, mods, layer, ln, w_in_r, qn, kvn, cos_t, sin_t, brow, g_sz, tt, act_dtype):
    nseq, t, _ = h.shape
    n = nseq * t
    m = g_sz * tt
    nj = t // tt
    grid = (nseq // g_sz, nj)
    rows = lambda w: pl.BlockSpec((m, w), lambda i, j: (i * nj + j, 0))
    out_shapes = (
        jax.ShapeDtypeStruct((n, ZA_W), F32),
        jax.ShapeDtypeStruct((n, Q_LORA), BF16),
        jax.ShapeDtypeStruct((n, KV_LORA), F32),
        jax.ShapeDtypeStruct((n, QK_CAT), act_dtype),
        jax.ShapeDtypeStruct((n, 3 * D_MODEL), BF16),
        jax.ShapeDtypeStruct((n, LANES), F32),
    )
    return pl.pallas_call(
        _inproj_kernel,
        out_shape=out_shapes,
        grid=grid,
        in_specs=[
            pl.BlockSpec((g_sz, tt, D_MODEL), lambda i, j: (i, j, 0)),
            _vec_spec(layer, D_MODEL),
            _mod_spec(layer, 3, g_sz),
            _mod_spec(layer, 4, g_sz),
            pl.BlockSpec((None, D_MODEL, N_IN_R), lambda i, j: (layer, 0, 0), pipeline_mode=pl.Buffered(1)),
            _vec_spec(layer, Q_LORA),
            _vec_spec(layer, KV_LORA),
            pl.BlockSpec((m, LANES), lambda i, j: (j, 0)),
            pl.BlockSpec((m, LANES), lambda i, j: (j, 0)),
            _vec_spec(layer, LANES),
        ],
        out_specs=(rows(ZA_W), rows(Q_LORA), rows(KV_LORA), rows(QK_CAT), rows(3 * D_MODEL), rows(LANES)),
        compiler_params=_cparams(2),
        name="inproj",
    )(h, ln, mods, mods, w_in_r, qn, kvn, cos_t, sin_t, brow)


def _conv_kernel(zb_ref, zc_ref, zx_ref, prev_ref, w_ref, a_ref, st_ref, carry_ref):
    j = pl.program_id(1)
    g_sz, tt, c = zc_ref.shape

    @pl.when(j == 0)
    def _():
        carry_ref[...] = prev_ref[...]

    cu = zc_ref[...] * zx_ref[...]
    prev = carry_ref[...]
    t = lax.broadcasted_iota(jnp.int32, cu.shape, 1)
    p0 = prev[:, 0:1, :]
    p1 = prev[:, 1:2, :]
    c1 = jnp.where(t == 0, p1, pltpu.roll(cu, 1, axis=1))
    c2 = jnp.where(t == 0, p0, jnp.where(t == 1, p1, pltpu.roll(cu, 2, axis=1)))
    w = w_ref[...]
    conv = w[0:1, :] * c2 + w[1:2, :] * c1 + w[2:3, :] * cu
    a = zb_ref[...] * conv
    a_ref[...] = a.reshape(g_sz * tt, c).astype(a_ref.dtype)
    new = cu[:, tt - (CONV_K - 1):tt, :]
    carry_ref[...] = new
    st_ref[...] = new


def _conv_call(za3, prev, conv_w_l, g_sz, tt):
    nseq, t, _ = za3.shape
    nj = t // tt
    c = CONV_WIDTH
    col = lambda b: pl.BlockSpec((g_sz, tt, c), lambda i, j: (i, j, b))
    st = pl.BlockSpec((g_sz, CONV_K - 1, c), lambda i, j: (i, 0, 0))
    return pl.pallas_call(
        _conv_kernel,
        out_shape=(jax.ShapeDtypeStruct((nseq * t, c), BF16),
                   jax.ShapeDtypeStruct((nseq, CONV_K - 1, c), F32)),
        grid=(nseq // g_sz, nj),
        in_specs=[col(0), col(1), col(2), st,
                  pl.BlockSpec((CONV_K, c), lambda i, j: (0, 0))],
        out_specs=(pl.BlockSpec((g_sz * tt, c), lambda i, j: (i * nj + j, 0)), st),
        scratch_shapes=[pltpu.VMEM((g_sz, CONV_K - 1, c), F32)],
        compiler_params=_cparams(2),
        name="conv_mixer",
    )(za3, za3, za3, prev, conv_w_l)


def _mlstm_kernel(q_ref, k_ref, v_ref, zo_ref, sm_ref, gt_ref, c0_ref, n0_ref, m0_ref, nw_ref,
                  a_ref, c_ref, n_ref, m_ref, c_s, n_s, m_s):
    j = pl.program_id(1)

    @pl.when(j == 0)
    def _():
        c_s[...] = c0_ref[...]
        n_s[...] = n0_ref[...]
        m_s[...] = m0_ref[...]

    for g in range(q_ref.shape[0]):
        a_out, c_new, n_new, m_new = _mlstm_chunk(
            q_ref[g], k_ref[g], v_ref[g], zo_ref[g], sm_ref[g], gt_ref[g], c_s[g], n_s[g], m_s[g], nw_ref[...])
        a_ref[g] = a_out.astype(a_ref.dtype)
        c_s[g] = c_new
        n_s[g] = n_new
        m_s[g] = m_new

    @pl.when(j == pl.num_programs(1) - 1)
    def _():
        c_ref[...] = c_s[...]
        n_ref[...] = n_s[...]
        m_ref[...] = m_s[...]


def _mlstm_chunk(q, k, v, zo, sm, gt, cmat, nrow, mrow, norm_w):
    lc = q.shape[0]
    w = M_WIDTH
    q = q * (M_DK ** -0.5)
    ri = lax.broadcasted_iota(jnp.int32, (lc, lc), 0)
    ci = lax.broadcasted_iota(jnp.int32, (lc, lc), 1)
    causal = ci <= ri
    bcol = jnp.dot(causal.astype(F32), sm, precision=lax.Precision.HIGHEST, preferred_element_type=F32)
    brow = jnp.dot(gt, (ri <= ci).astype(F32), precision=lax.Precision.HIGHEST, preferred_element_type=F32)

    lane = lax.broadcasted_iota(jnp.int32, (1, w), 1)
    lane_s = lax.broadcasted_iota(jnp.int32, (1, LANES), 1)
    kb = k.astype(BF16)
    vb = v.astype(BF16)
    qc = _bdot(q.astype(BF16), cmat.astype(BF16))

    zeros_l = jnp.zeros((lc, w), F32)
    num_intra = zeros_l
    den_l = zeros_l
    winter_l = zeros_l
    emt_l = zeros_l
    ws_l = zeros_l
    a_l = jnp.zeros((1, w), F32)
    m_new_row = mrow
    for h in range(M_HEADS):
        hm = (lane >= h * M_DV) & (lane < (h + 1) * M_DV)
        igc = sm[:, SM_IG + h:SM_IG + h + 1]
        bc = bcol[:, SM_LF + h:SM_LF + h + 1]
        igr = gt[SM_IG + h:SM_IG + h + 1, :]
        br = brow[SM_LF + h:SM_LF + h + 1, :]
        mp = mrow[:, h:h + 1]
        dlog = jnp.where(causal, bc - br + igr, NEG_INF)
        inter = bc + mp
        mt = jnp.maximum(inter, jnp.max(dlog, axis=1, keepdims=True))
        w_inter = jnp.exp(inter - mt)
        s = _bdot_nt(jnp.where(hm, q, 0.0).astype(BF16), kb) * jnp.exp(dlog - mt)
        num_intra = num_intra + _bdot(s.astype(BF16), jnp.where(hm, v, 0.0).astype(BF16))
        qn = jnp.sum(jnp.where(hm, q * nrow, 0.0), axis=1, keepdims=True)
        den = w_inter * qn + jnp.sum(s, axis=1, keepdims=True)
        den_l = jnp.where(hm, den, den_l)
        winter_l = jnp.where(hm, w_inter, winter_l)
        emt_l = jnp.where(hm, jnp.exp(-mt), emt_l)
        bl = bc[lc - 1:lc, :]
        glog = bl - bc + igc
        mn = jnp.maximum(bl + mp, jnp.max(glog, axis=0, keepdims=True))
        ws_l = jnp.where(hm, jnp.exp(glog - mn), ws_l)
        a_l = jnp.where(hm, jnp.exp(bl + mp - mn), a_l)
        m_new_row = jnp.where(lane_s == h, mn, m_new_row)

    num = winter_l * qc + num_intra
    hh = num / jnp.maximum(jnp.abs(den_l), emt_l)
    hsq = hh * hh
    ms_l = zeros_l
    for h in range(M_HEADS):
        hm = (lane >= h * M_DV) & (lane < (h + 1) * M_DV)
        ms = jnp.sum(jnp.where(hm, hsq, 0.0), axis=1, keepdims=True) * (1.0 / M_DV)
        ms_l = jnp.where(hm, ms, ms_l)
    hn = hh * lax.rsqrt(ms_l + EPS)
    a_out = hn * norm_w * jax.nn.sigmoid(zo)

    kw = k * ws_l
    upd = lax.dot_general(kw.astype(BF16), vb, (((0,), (0,)), ((), ())), preferred_element_type=F32)
    r2 = lax.broadcasted_iota(jnp.int32, (w, w), 0)
    c2 = lax.broadcasted_iota(jnp.int32, (w, w), 1)
    same_head = (r2 // M_DK) == (c2 // M_DV)
    c_new = a_l * cmat + jnp.where(same_head, upd, 0.0)
    n_new = a_l * nrow + jnp.sum(kw, axis=0, keepdims=True)
    return a_out, c_new, n_new, m_new_row


def _mlstm_call(za3, small3, gates_t, c0, n0, m0, norm_w, g_sz, lc, act_dtype):
    nseq, t, _ = za3.shape
    w = M_WIDTH
    col = lambda b: pl.BlockSpec((g_sz, lc, w), lambda i, j: (i, j, b))
    st_c = pl.BlockSpec((g_sz, w, w), lambda i, j: (i, 0, 0))
    st_n = pl.BlockSpec((g_sz, 1, w), lambda i, j: (i, 0, 0))
    st_m = pl.BlockSpec((g_sz, 1, LANES), lambda i, j: (i, 0, 0))
    return pl.pallas_call(
        _mlstm_kernel,
        out_shape=(jax.ShapeDtypeStruct((nseq, t, w), act_dtype),
                   jax.ShapeDtypeStruct((nseq, w, w), F32),
                   jax.ShapeDtypeStruct((nseq, 1, w), F32),
                   jax.ShapeDtypeStruct((nseq, 1, LANES), F32)),
        grid=(nseq // g_sz, t // lc),
        in_specs=[col(3), col(4), col(5), col(6),
                  pl.BlockSpec((g_sz, lc, LANES), lambda i, j: (i, j, 0)),
                  pl.BlockSpec((g_sz, 2 * M_HEADS, lc), lambda i, j: (i, 0, j)),
                  st_c, st_n, st_m,
                  pl.BlockSpec((1, w), lambda i, j: (0, 0))],
        out_specs=(pl.BlockSpec((g_sz, lc, w), lambda i, j: (i, j, 0)), st_c, st_n, st_m),
        scratch_shapes=[pltpu.VMEM((g_sz, w, w), F32), pltpu.VMEM((g_sz, 1, w), F32),
                        pltpu.VMEM((g_sz, 1, LANES), F32)],
        compiler_params=_cparams(2),
        name="mlstm",
    )(za3, za3, za3, za3, small3, gates_t, c0, n0, m0, norm_w)


def _qprep_kernel(cq_ref, wq_ref, wqs_ref, wuk_ref, cos_ref, sin_ref, q_ref):
    g_sz, _, tt, _ = q_ref.shape
    cq = cq_ref[...]
    qa = _bdot(cq, wq_ref[...])
    qs = _bdot(cq, wqs_ref[...])
    cos = cos_ref[...]
    sin = sin_ref[...]
    lane = lax.broadcasted_iota(jnp.int32, cos.shape, 1)
    is_rope = (lane >= SM_KR) & (lane < SM_KR + QK_ROPE)
    scale = (QK_NOPE + QK_ROPE) ** -0.5 * LOG2E
    for h in range(A_HEADS):
        slot = qa[:, h * Q_SLOT:(h + 1) * Q_SLOT] * cos + qs[:, h * Q_SLOT:(h + 1) * Q_SLOT] * sin
        q_lat = _bdot(slot.astype(BF16), wuk_ref[h]) * scale
        q_rope = jnp.where(is_rope, slot, 0.0) * scale
        q_ref[:, h, :, 0:KV_LORA] = q_lat.reshape(g_sz, tt, KV_LORA).astype(q_ref.dtype)
        q_ref[:, h, :, KV_LORA:QK_CAT] = q_rope.reshape(g_sz, tt, LANES).astype(q_ref.dtype)


def _qprep_call(cq, wq, wqs, wuk, cos_t, sin_t, layer, nseq, t, g_sz, tt, q_dtype):
    m = g_sz * tt
    nj = t // tt
    qw = A_HEADS * Q_SLOT
    return pl.pallas_call(
        _qprep_kernel,
        out_shape=jax.ShapeDtypeStruct((nseq, A_HEADS, t, QK_CAT), q_dtype),
        grid=(nseq // g_sz, nj),
        in_specs=[
            pl.BlockSpec((m, Q_LORA), lambda i, j: (i * nj + j, 0)),
            pl.BlockSpec((None, Q_LORA, qw), lambda i, j: (layer, 0, 0)),
            pl.BlockSpec((None, Q_LORA, qw), lambda i, j: (layer, 0, 0)),
            pl.BlockSpec((None, A_HEADS, Q_SLOT, KV_LORA), lambda i, j: (layer, 0, 0, 0)),
            pl.BlockSpec((m, LANES), lambda i, j: (j, 0)),
            pl.BlockSpec((m, LANES), lambda i, j: (j, 0)),
        ],
        out_specs=pl.BlockSpec((g_sz, A_HEADS, tt, QK_CAT), lambda i, j: (i, 0, j, 0)),
        compiler_params=_cparams(2),
        name="mla_qprep",
    )(cq, wq, wqs, wuk, cos_t, sin_t)


def _fold_lanes(x, op):
    out = x[:, 0:LANES]
    for j in range(1, x.shape[1] // LANES):
        out = op(out, x[:, j * LANES:(j + 1) * LANES])
    return out


def _attn_prompt_kernel(q_ref, k_ref, o_ref, s_ref, p_ref, m_ref, al_ref, l_ref, acc_ref, *, tk):
    qi = pl.program_id(1)
    tq = q_ref.shape[2]
    rows = A_HEADS * tq
    half = rows // 2
    hh = A_HEADS // 2
    rc = ATT_ROW_CHUNK
    n_lane_groups = tk // LANES
    m_ref[...] = jnp.full_like(m_ref, NEG_INF)
    l_ref[...] = jnp.zeros_like(l_ref)
    acc_ref[...] = jnp.zeros_like(acc_ref)
    col_minus_row = (lax.broadcasted_iota(jnp.int32, (rc, tk), 1)
                     - lax.broadcasted_iota(jnp.int32, (rc, tk), 0))

    def block(kb, masked, n_groups):
        nk = n_groups * LANES
        k = k_ref[pl.ds(pl.multiple_of(kb * tk, tk), nk), :]
        for b in range(2):
            qh = q_ref[0, b * hh:(b + 1) * hh].reshape(half, QK_CAT)
            s_ref[b * half:(b + 1) * half, 0:nk] = _bdot_nt(qh, k)
        for c in range(rows // rc):
            r = slice(c * rc, (c + 1) * rc)
            sc = s_ref[r, 0:nk]
            if masked:
                thr = qi * tq - kb * tk + (c * rc) % tq
                sc = jnp.where(col_minus_row[:, 0:nk] <= thr, sc, NEG_INF)
                s_ref[r, 0:nk] = sc
            m_prev = m_ref[r, :]
            m_new = jnp.maximum(m_prev, jnp.max(_fold_lanes(sc, jnp.maximum), axis=1, keepdims=True))
            al_ref[r, :] = jnp.exp2(m_prev - m_new)
            m_ref[r, :] = m_new
        for c in range(rows // rc):
            r = slice(c * rc, (c + 1) * rc)
            m_new = m_ref[r, :]
            part = None
            for j in range(n_groups):
                cols = slice(j * LANES, (j + 1) * LANES)
                p = jnp.exp2(s_ref[r, cols] - m_new)
                p_ref[r, cols] = p.astype(BF16)
                part = p if part is None else part + p
            l_ref[r, :] = al_ref[r, :] * l_ref[r, :] + part
        kv = k[:, 0:KV_LORA]
        for b in range(2):
            r = slice(b * half, (b + 1) * half)
            pv = _bdot(p_ref[r, 0:nk], kv)
            alpha = al_ref[r, :]
            for j in range(KV_LORA // LANES):
                cols = slice(j * LANES, (j + 1) * LANES)
                acc_ref[r, cols] = alpha * acc_ref[r, cols] + pv[:, cols]

    n_kb = (qi * tq + tq + tk - 1) // tk

    def body(kb, carry):
        block(kb, False, n_lane_groups)
        return carry

    lax.fori_loop(0, n_kb - 1, body, 0)
    tiles_per_block = tk // tq
    for v in range(tiles_per_block):
        @pl.when(qi % tiles_per_block == v)
        def _(v=v):
            block(n_kb - 1, True, ((v + 1) * tq + LANES - 1) // LANES)

    l_row = jnp.sum(l_ref[...], axis=1, keepdims=True)
    o_ref[0] = (acc_ref[...] / l_row).reshape(A_HEADS, tq, KV_LORA).astype(o_ref.dtype)


def _attn_prompt_call(q, kcat3):
    nseq, _, t, _ = q.shape
    tq = min(ATT_TQ, t)
    tk = min(ATT_TK, t)
    assert tk % tq == 0 and tq % ATT_ROW_CHUNK == 0 and t % tk == 0
    rows = A_HEADS * tq
    return pl.pallas_call(
        functools.partial(_attn_prompt_kernel, tk=tk),
        out_shape=jax.ShapeDtypeStruct((nseq, A_HEADS, t, KV_LORA), BF16),
        grid=(nseq, t // tq),
        in_specs=[pl.BlockSpec((1, A_HEADS, tq, QK_CAT), lambda i, j: (i, 0, j, 0)),
                  pl.BlockSpec((None, t, QK_CAT), lambda i, j: (i, 0, 0))],
        out_specs=pl.BlockSpec((1, A_HEADS, tq, KV_LORA), lambda i, j: (i, 0, j, 0)),
        scratch_shapes=[pltpu.VMEM((rows, tk), F32), pltpu.VMEM((rows, tk), BF16),
                        pltpu.VMEM((rows, LANES), F32), pltpu.VMEM((rows, LANES), F32),
                        pltpu.VMEM((rows, LANES), F32), pltpu.VMEM((rows, KV_LORA), F32)],
        compiler_params=_cparams(2),
        name="attn_prompt",
    )(q, kcat3)


def _attn_sample_kernel(pt_ref, q_ref, knew_ref, lat_hbm, krt_hbm, o_ref, latbuf, krtbuf, kbuf, krbuf, sems,
                        *, layer, n_pages):
    step = pl.program_id(0)
    n_seq = pl.num_programs(0)
    slot = step % 2
    nxt = jnp.minimum(step + 1, n_seq - 1)
    dec = q_ref.shape[2]
    rows = A_HEADS * dec

    def page_copies(seq, slot_):
        copies = []
        for i in range(n_pages):
            page = 0 if seq is None else pt_ref[seq, i]
            keys = pl.ds(i * PAGE_SIZE, PAGE_SIZE)
            copies.append(pltpu.make_async_copy(lat_hbm.at[layer, page], latbuf.at[slot_, keys, :],
                                                sems.at[0, slot_]))
            copies.append(pltpu.make_async_copy(krt_hbm.at[layer, page], krtbuf.at[slot_, :, keys],
                                                sems.at[1, slot_]))
        return copies

    @pl.when(step == 0)
    def _():
        for c in page_copies(0, 0):
            c.start()

    for c in page_copies(nxt, 1 - slot):
        c.start()
    for c in page_copies(None, slot):
        c.wait()

    for i in range(n_pages):
        keys = slice(i * PAGE_SIZE, (i + 1) * PAGE_SIZE)
        kbuf[keys, :] = latbuf[slot, keys, :].astype(BF16)
        krbuf[:, keys] = krtbuf[slot, :, keys].astype(BF16)
    q = q_ref[0].reshape(rows, QK_CAT)
    q_lat = q[:, 0:KV_LORA].astype(BF16)
    q_rope = q[:, KV_LORA + SM_KR:KV_LORA + SM_KR + QK_ROPE].astype(BF16)
    kv = kbuf[...]
    s = _bdot_nt(q_lat, kv) + _bdot(q_rope, krbuf[...])
    knew = knew_ref[0].astype(BF16)
    s2 = _bdot_nt(q.astype(BF16), knew)
    row = lax.broadcasted_iota(jnp.int32, s2.shape, 0)
    col = lax.broadcasted_iota(jnp.int32, s2.shape, 1)
    s2 = jnp.where(col <= (row & (dec - 1)), s2, NEG_INF)
    m = jnp.maximum(jnp.max(s, axis=1, keepdims=True), jnp.max(s2, axis=1, keepdims=True))
    p = jnp.exp2(s - m)
    p2 = jnp.exp2(s2 - m)
    l = jnp.sum(p, axis=1, keepdims=True) + jnp.sum(p2, axis=1, keepdims=True)
    o = _bdot(p.astype(BF16), kv) + _bdot(p2.astype(BF16), knew[:, 0:KV_LORA])
    o_ref[0] = (o / l).reshape(A_HEADS, dec, KV_LORA)

    @pl.when(step == n_seq - 1)
    def _():
        for c in page_copies(None, 1 - slot):
            c.wait()


def _attn_sample_call(page_table, q, kcat3, cache_latent, cache_krope_t, layer):
    nseq, _, dec, _ = q.shape
    n_pages = page_table.shape[1]
    n_keys = n_pages * PAGE_SIZE
    assert dec & (dec - 1) == 0
    grid_spec = pltpu.PrefetchScalarGridSpec(
        num_scalar_prefetch=1,
        grid=(nseq,),
        in_specs=[pl.BlockSpec((1, A_HEADS, dec, QK_CAT), lambda s, pt: (s, 0, 0, 0)),
                  pl.BlockSpec((1, dec, QK_CAT), lambda s, pt: (s, 0, 0)),
                  pl.BlockSpec(memory_space=pl.ANY),
                  pl.BlockSpec(memory_space=pl.ANY)],
        out_specs=pl.BlockSpec((1, A_HEADS, dec, KV_LORA), lambda s, pt: (s, 0, 0, 0)),
        scratch_shapes=[pltpu.VMEM((2, n_keys, KV_LORA), F32),
                        pltpu.VMEM((2, QK_ROPE, n_keys), F32),
                        pltpu.VMEM((n_keys, KV_LORA), BF16),
                        pltpu.VMEM((QK_ROPE, n_keys), BF16),
                        pltpu.SemaphoreType.DMA((2, 2))],
    )
    return pl.pallas_call(
        functools.partial(_attn_sample_kernel, layer=layer, n_pages=n_pages),
        out_shape=jax.ShapeDtypeStruct((nseq, A_HEADS, dec, KV_LORA), F32),
        grid_spec=grid_spec,
        compiler_params=_cparams(1),
        name="attn_sample",
    )(page_table, q, kcat3, cache_latent, cache_krope_t)


def _merge_kernel(h_ref, g2_ref, ac_ref, am_ref, ol_ref, gates_ref, wc_ref, wm_ref, wuv_ref, wo_ref, wmix_ref,
                  o_ref):
    g_sz, tt, _ = h_ref.shape
    m = g_sz * tt
    y_conv = _bdot(ac_ref[...].astype(BF16), wc_ref[...])
    y_m = _bdot(am_ref[...].astype(BF16), wm_ref[...])
    parts = []
    for h in range(A_HEADS):
        ol = ol_ref[:, h].reshape(m, KV_LORA).astype(BF16)
        parts.append(_bdot(ol, wuv_ref[h]))
    o = jnp.concatenate(parts, axis=-1).astype(BF16)
    y_a = _bdot(o, wo_ref[...])
    gate = lambda b: gates_ref[:, b * D_MODEL:(b + 1) * D_MODEL].astype(F32)
    merged = gate(0) * y_conv + gate(1) * y_m + gate(2) * y_a
    mixed = _bdot(merged.astype(BF16), wmix_ref[...])
    o_ref[...] = h_ref[...] + g2_ref[...] * mixed.reshape(g_sz, tt, D_MODEL)


def _merge_call(h, mods, layer, a_conv, a_m, o_lat, gates, wc, wm, wuv, wo, wmix, g_sz, tt):
    nseq, t, _ = h.shape
    m = g_sz * tt
    nj = t // tt
    rows = lambda w: pl.BlockSpec((m, w), lambda i, j: (i * nj + j, 0))
    row3 = pl.BlockSpec((g_sz, tt, D_MODEL), lambda i, j: (i, j, 0))
    wspec = lambda a, b: pl.BlockSpec((None, a, b), lambda i, j: (layer, 0, 0))
    return pl.pallas_call(
        _merge_kernel,
        out_shape=jax.ShapeDtypeStruct(h.shape, F32),
        grid=(nseq // g_sz, nj),
        in_specs=[
            row3,
            _mod_spec(layer, 5, g_sz),
            rows(CONV_WIDTH), rows(M_WIDTH),
            pl.BlockSpec((g_sz, A_HEADS, tt, KV_LORA), lambda i, j: (i, 0, j, 0)),
            rows(3 * D_MODEL),
            wspec(CONV_WIDTH, D_MODEL), wspec(M_WIDTH, D_MODEL),
            pl.BlockSpec((None, A_HEADS, KV_LORA, V_HEAD), lambda i, j: (layer, 0, 0, 0)),
            wspec(A_HEADS * V_HEAD, D_MODEL), wspec(D_MODEL, D_MODEL),
        ],
        out_specs=row3,
        compiler_params=_cparams(2),
        name="merge",
    )(h, mods, a_conv, a_m, o_lat, gates, wc, wm, wuv, wo, wmix)


def _rope_tables(pos):
    half = QK_ROPE // 2
    inv = ROPE_BASE ** (-jnp.arange(half, dtype=F32) * 2.0 / QK_ROPE)
    ang = pos.astype(F32)[:, None] * inv[None, :]
    cos, sin = jnp.cos(ang), jnp.sin(ang)
    n = pos.shape[0]
    ones = jnp.ones((n, SM_KR), F32)
    tail = LANES - SM_KR - QK_ROPE
    cos_t = jnp.concatenate([ones, cos, cos, jnp.ones((n, tail), F32)], axis=1)
    sin_t = jnp.concatenate([jnp.zeros((n, SM_KR), F32), -sin, sin, jnp.zeros((n, tail), F32)], axis=1)
    return cos_t, sin_t


def _rearranged_w_in(w_in):
    depth = w_in.shape[0]
    off = np.cumsum((0,) + (CONV_WIDTH,) * 3 + (M_WIDTH,) * 4 + (M_HEADS,) * 2
                    + (Q_LORA, KV_LORA, QK_ROPE) + (D_MODEL,) * 3)
    zi = w_in[:, :, off[7]:off[8]]
    zf = w_in[:, :, off[8]:off[9]]
    zdq = w_in[:, :, off[9]:off[10]]
    zdkv = w_in[:, :, off[10]:off[11]]
    zkr = w_in[:, :, off[11]:off[12]]
    gates = w_in[:, :, off[12]:off[15]]
    zeros = lambda n: jnp.zeros((depth, D_MODEL, n), w_in.dtype)
    half = QK_ROPE // 2
    small_a = jnp.concatenate([zi, zf, zeros(SM_KR - 2 * M_HEADS), zkr, zeros(LANES - SM_KR - QK_ROPE)], axis=2)
    small_b = jnp.concatenate([zeros(SM_KR), zkr[:, :, half:], zkr[:, :, :half],
                               zeros(LANES - SM_KR - QK_ROPE)], axis=2)
    return jnp.concatenate([w_in[:, :, 0:off[7]], zdq, zdkv, gates, small_a, small_b], axis=2).astype(BF16)


def _rearranged_w_uq(w_uq):
    depth = w_uq.shape[0]
    per = QK_NOPE + QK_ROPE
    half = QK_ROPE // 2
    w = w_uq.reshape(depth, Q_LORA, A_HEADS, per)
    nope = w[..., :QK_NOPE]
    r1 = w[..., QK_NOPE:QK_NOPE + half]
    r2 = w[..., QK_NOPE + half:]
    pad = jnp.zeros((depth, Q_LORA, A_HEADS, Q_SLOT - per), w_uq.dtype)
    wq = jnp.concatenate([nope, r1, r2, pad], axis=-1)
    wqs = jnp.concatenate([jnp.zeros_like(nope), r2, r1, pad], axis=-1)
    flat = lambda a: a.reshape(depth, Q_LORA, A_HEADS * Q_SLOT).astype(BF16)
    return flat(wq), flat(wqs)


def _block_diag_state(c):
    nseq = c.shape[0]
    eye = jnp.eye(M_HEADS, dtype=c.dtype)
    return jnp.einsum('bhde,hg->bhdge', c, eye).reshape(nseq, M_HEADS * M_DK, M_HEADS * M_DV)


def _diag_blocks(c_bd):
    nseq = c_bd.shape[0]
    c5 = c_bd.reshape(nseq, M_HEADS, M_DK, M_HEADS, M_DV)
    return jnp.stack([c5[:, h, :, h, :] for h in range(M_HEADS)], axis=1)


def _tile_plan(nseq, t):
    if t >= MLSTM_CHUNK_P:
        blk = lambda rows: (1, min(rows, t))
        act = BF16
    else:
        blk = lambda rows: (min(max(rows // t, 1), nseq), t)
        act = F32
    plan = dict(ffn=blk(512), inproj=blk(256), conv=blk(1024 if act == F32 else 512), qprep=blk(512),
                merge=blk(512), mlstm=(min(MLSTM_SEQS_P if act == BF16 else MLSTM_SEQS_S, nseq), min(MLSTM_CHUNK_P, t)),
                act=act)
    for name in ('ffn', 'inproj', 'conv', 'qprep', 'merge', 'mlstm'):
        g_sz, tt = plan[name]
        assert nseq % g_sz == 0 and t % tt == 0, (name, nseq, t)
    return plan


def _layer_group(x, mods, layer, wts, conv_prev, c0, n0, m0, cos_t, sin_t, attend, final):
    nseq, t, _ = x.shape
    plan = _tile_plan(nseq, t)
    act_dtype = plan['act']
    h = _ffn_call(x, mods, layer, True, wts['ln_ffn1'], wts['ffn1_w13'], wts['ffn1_w2'], wts['final_norm'],
                  False, *plan['ffn'])
    za, cq, ckv, kcat, gates, small = _inproj_call(
        h, mods, layer, wts['ln_mix'], wts['w_in_r'], wts['mla_q_norm'], wts['mla_kv_norm'],
        cos_t, sin_t, wts['brow'], *plan['inproj'], act_dtype)
    za3 = za.reshape(nseq, t, ZA_W)
    small3 = small.reshape(nseq, t, LANES)
    a_conv, conv_new = _conv_call(za3, conv_prev, wts['conv_w'][layer], *plan['conv'])
    gates_t = jnp.swapaxes(small3[:, :, 0:2 * M_HEADS], 1, 2)
    a_m, c_bd, n1, m1 = _mlstm_call(za3, small3, gates_t, c0, n0, m0, wts['mlstm_norm'][layer], *plan['mlstm'],
                                    act_dtype)
    a_m = a_m.reshape(nseq * t, M_WIDTH)
    q = _qprep_call(cq, wts['wq'], wts['wqs'], wts['wuk'], cos_t, sin_t, layer, nseq, t, *plan['qprep'], act_dtype)
    o_lat = attend(q, kcat.reshape(nseq, t, QK_CAT))
    h2 = _merge_call(h, mods, layer, a_conv, a_m, o_lat, gates, wts['conv_out'], wts['mlstm_out'], wts['wuv'],
                     wts['mla_w_o'], wts['w_mix_out'], *plan['merge'])
    out = _ffn_call(h2, mods, layer, False, wts['ln_ffn2'], wts['ffn2_w13'], wts['ffn2_w2'], wts['final_norm'],
                    final, *plan['ffn'])
    state = (ckv.reshape(nseq, t, KV_LORA), small3[:, :, SM_KR:SM_KR + QK_ROPE], conv_new,
             _diag_blocks(c_bd), n1.reshape(nseq, M_HEADS, M_DK), m1[:, 0, 0:M_HEADS])
    return out, state


def kernel(x_prompt, x_sample, cache_latent, cache_krope, state_conv, state_mlstm_C, state_mlstm_n, state_mlstm_m, page_table, c_prompt, c_sample, ln_ffn1, ffn1_w13, ffn1_w2, ln_mix, w_in, conv_w, conv_out, mlstm_b_i, mlstm_b_f, mlstm_norm, mlstm_out, mla_q_norm, mla_w_uq, mla_w_uk, mla_kv_norm, mla_w_uv, mla_w_o, w_mix_out, ln_ffn2, ffn2_w13, ffn2_w2, w_ada, b_ada, final_norm):
    bp, seq, _ = x_prompt.shape
    bs, dec, _ = x_sample.shape
    past_len = page_table.shape[1] * PAGE_SIZE

    c_all = jnp.concatenate([c_prompt, c_sample], axis=0)
    mods = _mods_call(c_all, w_ada, b_ada)
    mods_p = mods[:, :, :bp].reshape(DEPTH, 9, bp, 1, D_MODEL)
    mods_s = mods[:, :, bp:].reshape(DEPTH, 9, bs, 1, D_MODEL)

    wq, wqs = _rearranged_w_uq(mla_w_uq)
    wuk = mla_w_uk.reshape(DEPTH, KV_LORA, A_HEADS, QK_NOPE).transpose(0, 2, 3, 1)
    wuk = jnp.pad(wuk, ((0, 0), (0, 0), (0, Q_SLOT - QK_NOPE), (0, 0))).astype(BF16)
    brow = jnp.concatenate([mlstm_b_i, mlstm_b_f, jnp.zeros((DEPTH, LANES - 2 * M_HEADS), F32)], axis=1)
    wts = {
        'ln_ffn1': ln_ffn1.reshape(DEPTH, 1, D_MODEL), 'ffn1_w13': ffn1_w13.astype(BF16), 'ffn1_w2': ffn1_w2.astype(BF16),
        'ln_mix': ln_mix.reshape(DEPTH, 1, D_MODEL), 'w_in_r': _rearranged_w_in(w_in), 'conv_w': conv_w,
        'conv_out': conv_out.astype(BF16), 'brow': brow.reshape(DEPTH, 1, LANES),
        'mlstm_norm': mlstm_norm.reshape(DEPTH, 1, M_WIDTH), 'mlstm_out': mlstm_out.astype(BF16),
        'mla_q_norm': mla_q_norm.reshape(DEPTH, 1, Q_LORA), 'wq': wq, 'wqs': wqs, 'wuk': wuk, 'mla_kv_norm': mla_kv_norm.reshape(DEPTH, 1, KV_LORA),
        'wuv': mla_w_uv.reshape(DEPTH, KV_LORA, A_HEADS, V_HEAD).transpose(0, 2, 1, 3).astype(BF16),
        'mla_w_o': mla_w_o.astype(BF16), 'w_mix_out': w_mix_out.astype(BF16),
        'ln_ffn2': ln_ffn2.reshape(DEPTH, 1, D_MODEL), 'ffn2_w13': ffn2_w13.astype(BF16), 'ffn2_w2': ffn2_w2.astype(BF16),
        'final_norm': final_norm.reshape(1, D_MODEL),
    }

    cos_p, sin_p = _rope_tables(jnp.arange(seq))
    cos_s, sin_s = _rope_tables(past_len + jnp.arange(dec))
    cos_s = jnp.tile(cos_s, (bs, 1))
    sin_s = jnp.tile(sin_s, (bs, 1))

    cache_krope_t = jnp.swapaxes(cache_krope, 2, 3)

    zero_conv = jnp.zeros((bp, CONV_K - 1, CONV_WIDTH), F32)
    zero_c = jnp.zeros((bp, M_WIDTH, M_WIDTH), F32)
    zero_n = jnp.zeros((bp, 1, M_WIDTH), F32)
    zero_m = jnp.zeros((bp, 1, LANES), F32)

    xp, xs = x_prompt, x_sample
    p_st, s_st = [], []
    for layer in range(DEPTH):
        final = layer == DEPTH - 1
        xp, sp = _layer_group(xp, mods_p, layer, wts, zero_conv, zero_c, zero_n, zero_m,
                              cos_p, sin_p, _attn_prompt_call, final)
        p_st.append(sp)
        m0 = jnp.pad(state_mlstm_m[layer], ((0, 0), (0, LANES - M_HEADS))).reshape(bs, 1, LANES)
        attend_s = functools.partial(_attn_sample_call, page_table, cache_latent=cache_latent,
                                     cache_krope_t=cache_krope_t, layer=layer)
        xs, ss = _layer_group(xs, mods_s, layer, wts, state_conv[layer],
                              _block_diag_state(state_mlstm_C[layer]),
                              state_mlstm_n[layer].reshape(bs, 1, M_WIDTH), m0,
                              cos_s, sin_s, attend_s, final)
        s_st.append(ss)
    p_out = tuple(jnp.stack(a) for a in zip(*p_st))
    s_out = tuple(jnp.stack(a) for a in zip(*s_st))
    return (xp, xs) + p_out + s_out
```

```python
import functools

import jax
import jax.numpy as jnp
import numpy as np
from jax import lax
from jax.experimental import pallas as pl
from jax.experimental.pallas import tpu as pltpu

D_MODEL = 1024
DEPTH = 4
PAGE_SIZE = 128
D_FF = 2816
CONV_WIDTH = 256
CONV_K = 3
M_HEADS = 4
M_DK = 64
M_DV = 64
M_WIDTH = M_HEADS * M_DV
A_HEADS = 8
QK_NOPE = 64
QK_ROPE = 32
V_HEAD = 64
Q_LORA = 384
KV_LORA = 256
ROPE_BASE = 10000.0
EPS = 1e-6

LANES = 128
VMEM_LIMIT = 56 * 1024 * 1024

F32 = jnp.float32
BF16 = jnp.bfloat16
NEG_INF = float("-inf")
LOG2E = 1.4426950408889634

ZA_W = 7 * 256
OFF_DQ = ZA_W
OFF_DKV = OFF_DQ + Q_LORA
OFF_GATES = OFF_DKV + KV_LORA
OFF_SMALL_A = OFF_GATES + 3 * D_MODEL
OFF_SMALL_B = OFF_SMALL_A + LANES
N_IN_R = OFF_SMALL_B + LANES
SM_IG = 0
SM_LF = M_HEADS
SM_KR = 64
QK_CAT = KV_LORA + LANES
Q_SLOT = LANES

FFN_CHUNKS = (768, 768, 768, 512)
assert sum(FFN_CHUNKS) == D_FF
ATT_TQ = 256
ATT_TK = 512
ATT_ROW_CHUNK = 32
MLSTM_CHUNK_P = 256
MLSTM_SEQS_P = 1
MLSTM_SEQS_S = 8


def _cparams(n_axes):
    return pltpu.CompilerParams(dimension_semantics=("arbitrary",) * n_axes,
                                vmem_limit_bytes=VMEM_LIMIT)


def _bdot(a, b):
    return jnp.dot(a, b, preferred_element_type=F32)


def _bdot_nt(a, b):
    return lax.dot_general(a, b, (((1,), (1,)), ((), ())), preferred_element_type=F32)


def _rms(x):
    return x * lax.rsqrt(jnp.mean(x * x, axis=-1, keepdims=True) + EPS)


def _mods_kernel(c_ref, w_ref, b_ref, o_ref):
    c = c_ref[...]
    a = (c * jax.nn.sigmoid(c)).astype(BF16)
    o_ref[...] = _bdot(a, w_ref[...].astype(BF16)) + b_ref[...]


def _mods_call(c_all, w_ada, b_ada):
    nb = c_all.shape[0]
    return pl.pallas_call(
        _mods_kernel,
        out_shape=jax.ShapeDtypeStruct((DEPTH, 9, nb, D_MODEL), F32),
        grid=(DEPTH, 9),
        in_specs=[
            pl.BlockSpec((nb, D_MODEL), lambda l, j: (0, 0)),
            pl.BlockSpec((None, D_MODEL, D_MODEL), lambda l, j: (l, 0, j)),
            pl.BlockSpec((None, 1, D_MODEL), lambda l, j: (l, 0, j)),
        ],
        out_specs=pl.BlockSpec((None, None, nb, D_MODEL), lambda l, j: (l, j, 0, 0)),
        compiler_params=_cparams(2),
        name="adaln_mods",
    )(c_all, w_ada, b_ada.reshape(DEPTH, 1, 9 * D_MODEL))


def _vec_spec(layer, width):
    return pl.BlockSpec((None, 1, width), lambda *_: (layer, 0, 0))


def _mod_spec(layer, which, g):
    return pl.BlockSpec((None, None, g, 1, D_MODEL), lambda i, *_: (layer, which, i, 0, 0))


def _ffn_kernel(x_ref, ln_ref, sh_ref, sc_ref, g_ref, w13_ref, w2_ref, fn_ref, o_ref, *, final):
    g_sz, tt, _ = x_ref.shape
    x = x_ref[...]
    u = (_rms(x) * ln_ref[...]) * (1 + sc_ref[...]) + sh_ref[...]
    xn = u.reshape(g_sz * tt, D_MODEL).astype(BF16)
    y = None
    start = 0
    for width in FFN_CHUNKS:
        cols = slice(start, start + width)
        gate = _bdot(xn, w13_ref[:, cols])
        up = _bdot(xn, w13_ref[:, D_FF + start:D_FF + start + width])
        start += width
        act = (gate * jax.nn.sigmoid(gate) * up).astype(BF16)
        part = _bdot(act, w2_ref[cols, :])
        y = part if y is None else y + part
    out = x + (0.5 * g_ref[...]) * y.reshape(g_sz, tt, D_MODEL)
    if final:
        out = _rms(out) * fn_ref[...]
    o_ref[...] = out


def _ffn_call(x, mods, layer, first, ln, w13, w2, fnorm, final, g_sz, tt):
    nseq, t, _ = x.shape
    base = 0 if first else 6
    row = pl.BlockSpec((g_sz, tt, D_MODEL), lambda i, j: (i, j, 0))
    resident = lambda a, b: pl.BlockSpec((None, a, b), lambda i, j: (layer, 0, 0), pipeline_mode=pl.Buffered(1))
    return pl.pallas_call(
        functools.partial(_ffn_kernel, final=final),
        out_shape=jax.ShapeDtypeStruct(x.shape, F32),
        grid=(nseq // g_sz, t // tt),
        in_specs=[
            row,
            _vec_spec(layer, D_MODEL),
            _mod_spec(layer, base + 0, g_sz),
            _mod_spec(layer, base + 1, g_sz),
            _mod_spec(layer, base + 2, g_sz),
            resident(D_MODEL, 2 * D_FF),
            resident(D_FF, D_MODEL),
            pl.BlockSpec((1, D_MODEL), lambda i, j: (0, 0)),
        ],
        out_specs=row,
        compiler_params=_cparams(2),
        name="ffn",
    )(x, ln, mods, mods, mods, w13, w2, fnorm)


def _inproj_kernel(h_ref, ln_ref, sh_ref, sc_ref, w_ref, qn_ref, kvn_ref, cos_ref, sin_ref, brow_ref,
                   za_ref, cq_ref, ckv_ref, kcat_ref, gates_ref, small_ref):
    g_sz, tt, _ = h_ref.shape
    m = g_sz * tt
    y = _rms(h_ref[...]) * ln_ref[...]
    u = y * (1 + sc_ref[...]) + sh_ref[...]
    xn = u.reshape(m, D_MODEL).astype(BF16)

    za_ref[...] = _bdot(xn, w_ref[:, 0:ZA_W])
    zdq = _bdot(xn, w_ref[:, OFF_DQ:OFF_DKV])
    cq_ref[...] = (_rms(zdq) * qn_ref[...]).astype(cq_ref.dtype)
    zdkv = _bdot(xn, w_ref[:, OFF_DKV:OFF_GATES])
    ckv = _rms(zdkv) * kvn_ref[...]
    ckv_ref[...] = ckv
    gates_ref[...] = jax.nn.sigmoid(_bdot(xn, w_ref[:, OFF_GATES:OFF_SMALL_A])).astype(gates_ref.dtype)

    sa = _bdot(xn, w_ref[:, OFF_SMALL_A:OFF_SMALL_B])
    sb = _bdot(xn, w_ref[:, OFF_SMALL_B:N_IN_R])
    rot = sa * cos_ref[...] + sb * sin_ref[...]
    lane = lax.broadcasted_iota(jnp.int32, rot.shape, 1)
    zb = rot + brow_ref[...]
    logsig = jnp.minimum(zb, 0.0) - jnp.log1p(jnp.exp(-jnp.abs(zb)))
    is_kr = (lane >= SM_KR) & (lane < SM_KR + QK_ROPE)
    small = jnp.where(lane < SM_LF, zb,
                      jnp.where(lane < SM_LF + M_HEADS, logsig,
                                jnp.where(is_kr, rot, 0.0)))
    small_ref[...] = small
    kcat_ref[:, 0:KV_LORA] = ckv.astype(kcat_ref.dtype)
    kcat_ref[:, KV_LORA:QK_CAT] = jnp.where(is_kr, rot, 0.0).astype(kcat_ref.dtype)


def _inproj_call(h, mods, layer, ln, w_in_r, qn, kvn, cos_t, sin_t, brow, g_sz, tt, act_dtype):
    nseq, t, _ = h.shape
    n = nseq * t
    m = g_sz * tt
    nj = t // tt
    grid = (nseq // g_sz, nj)
    rows = lambda w: pl.BlockSpec((m, w), lambda i, j: (i * nj + j, 0))
    out_shapes = (
        jax.ShapeDtypeStruct((n, ZA_W), F32),
        jax.ShapeDtypeStruct((n, Q_LORA), BF16),
        jax.ShapeDtypeStruct((n, KV_LORA), F32),
        jax.ShapeDtypeStruct((n, QK_CAT), act_dtype),
        jax.ShapeDtypeStruct((n, 3 * D_MODEL), BF16),
        jax.ShapeDtypeStruct((n, LANES), F32),
    )
    return pl.pallas_call(
        _inproj_kernel,
        out_shape=out_shapes,
        grid=grid,
        in_specs=[
            pl.BlockSpec((g_sz, tt, D_MODEL), lambda i, j: (i, j, 0)),
            _vec_spec(layer, D_MODEL),
            _mod_spec(layer, 3, g_sz),
            _mod_spec(layer, 4, g_sz),
            pl.BlockSpec((None, D_MODEL, N_IN_R), lambda i, j: (layer, 0, 0), pipeline_mode=pl.Buffered(1)),
            _vec_spec(layer, Q_LORA),
            _vec_spec(layer, KV_LORA),
            pl.BlockSpec((m, LANES), lambda i, j: (j, 0)),
            pl.BlockSpec((m, LANES), lambda i, j: (j, 0)),
            _vec_spec(layer, LANES),
        ],
        out_specs=(rows(ZA_W), rows(Q_LORA), rows(KV_LORA), rows(QK_CAT), rows(3 * D_MODEL), rows(LANES)),
        compiler_params=_cparams(2),
        name="inproj",
    )(h, ln, mods, mods, w_in_r, qn, kvn, cos_t, sin_t, brow)


def _conv_kernel(zb_ref, zc_ref, zx_ref, prev_ref, w_ref, a_ref, st_ref, carry_ref):
    j = pl.program_id(1)
    g_sz, tt, c = zc_ref.shape

    @pl.when(j == 0)
    def _():
        carry_ref[...] = prev_ref[...]

    cu = zc_ref[...] * zx_ref[...]
    prev = carry_ref[...]
    t = lax.broadcasted_iota(jnp.int32, cu.shape, 1)
    p0 = prev[:, 0:1, :]
    p1 = prev[:, 1:2, :]
    c1 = jnp.where(t == 0, p1, pltpu.roll(cu, 1, axis=1))
    c2 = jnp.where(t == 0, p0, jnp.where(t == 1, p1, pltpu.roll(cu, 2, axis=1)))
    w = w_ref[...]
    conv = w[0:1, :] * c2 + w[1:2, :] * c1 + w[2:3, :] * cu
    a = zb_ref[...] * conv
    a_ref[...] = a.reshape(g_sz * tt, c).astype(a_ref.dtype)
    new = cu[:, tt - (CONV_K - 1):tt, :]
    carry_ref[...] = new
    st_ref[...] = new


def _conv_call(za3, prev, conv_w_l, g_sz, tt):
    nseq, t, _ = za3.shape
    nj = t // tt
    c = CONV_WIDTH
    col = lambda b: pl.BlockSpec((g_sz, tt, c), lambda i, j: (i, j, b))
    st = pl.BlockSpec((g_sz, CONV_K - 1, c), lambda i, j: (i, 0, 0))
    return pl.pallas_call(
        _conv_kernel,
        out_shape=(jax.ShapeDtypeStruct((nseq * t, c), BF16),
                   jax.ShapeDtypeStruct((nseq, CONV_K - 1, c), F32)),
        grid=(nseq // g_sz, nj),
        in_specs=[col(0), col(1), col(2), st,
                  pl.BlockSpec((CONV_K, c), lambda i, j: (0, 0))],
        out_specs=(pl.BlockSpec((g_sz * tt, c), lambda i, j: (i * nj + j, 0)), st),
        scratch_shapes=[pltpu.VMEM((g_sz, CONV_K - 1, c), F32)],
        compiler_params=_cparams(2),
        name="conv_mixer",
    )(za3, za3, za3, prev, conv_w_l)


def _mlstm_kernel(q_ref, k_ref, v_ref, zo_ref, sm_ref, gt_ref, c0_ref, n0_ref, m0_ref, nw_ref,
                  a_ref, c_ref, n_ref, m_ref, c_s, n_s, m_s):
    j = pl.program_id(1)

    @pl.when(j == 0)
    def _():
        c_s[...] = c0_ref[...]
        n_s[...] = n0_ref[...]
        m_s[...] = m0_ref[...]

    for g in range(q_ref.shape[0]):
        a_out, c_new, n_new, m_new = _mlstm_chunk(
            q_ref[g], k_ref[g], v_ref[g], zo_ref[g], sm_ref[g], gt_ref[g], c_s[g], n_s[g], m_s[g], nw_ref[...])
        a_ref[g] = a_out.astype(a_ref.dtype)
        c_s[g] = c_new
        n_s[g] = n_new
        m_s[g] = m_new

    @pl.when(j == pl.num_programs(1) - 1)
    def _():
        c_ref[...] = c_s[...]
        n_ref[...] = n_s[...]
        m_ref[...] = m_s[...]


def _mlstm_chunk(q, k, v, zo, sm, gt, cmat, nrow, mrow, norm_w):
    lc = q.shape[0]
    w = M_WIDTH
    q = q * (M_DK ** -0.5)
    ri = lax.broadcasted_iota(jnp.int32, (lc, lc), 0)
    ci = lax.broadcasted_iota(jnp.int32, (lc, lc), 1)
    causal = ci <= ri
    bcol = jnp.dot(causal.astype(F32), sm, precision=lax.Precision.HIGHEST, preferred_element_type=F32)
    brow = jnp.dot(gt, (ri <= ci).astype(F32), precision=lax.Precision.HIGHEST, preferred_element_type=F32)

    lane = lax.broadcasted_iota(jnp.int32, (1, w), 1)
    lane_s = lax.broadcasted_iota(jnp.int32, (1, LANES), 1)
    kb = k.astype(BF16)
    vb = v.astype(BF16)
    qc = _bdot(q.astype(BF16), cmat.astype(BF16))

    zeros_l = jnp.zeros((lc, w), F32)
    num_intra = zeros_l
    den_l = zeros_l
    winter_l = zeros_l
    emt_l = zeros_l
    ws_l = zeros_l
    a_l = jnp.zeros((1, w), F32)
    m_new_row = mrow
    for h in range(M_HEADS):
        hm = (lane >= h * M_DV) & (lane < (h + 1) * M_DV)
        igc = sm[:, SM_IG + h:SM_IG + h + 1]
        bc = bcol[:, SM_LF + h:SM_LF + h + 1]
        igr = gt[SM_IG + h:SM_IG + h + 1, :]
        br = brow[SM_LF + h:SM_LF + h + 1, :]
        mp = mrow[:, h:h + 1]
        dlog = jnp.where(causal, bc - br + igr, NEG_INF)
        inter = bc + mp
        mt = jnp.maximum(inter, jnp.max(dlog, axis=1, keepdims=True))
        w_inter = jnp.exp(inter - mt)
        s = _bdot_nt(jnp.where(hm, q, 0.0).astype(BF16), kb) * jnp.exp(dlog - mt)
        num_intra = num_intra + _bdot(s.astype(BF16), jnp.where(hm, v, 0.0).astype(BF16))
        qn = jnp.sum(jnp.where(hm, q * nrow, 0.0), axis=1, keepdims=True)
        den = w_inter * qn + jnp.sum(s, axis=1, keepdims=True)
        den_l = jnp.where(hm, den, den_l)
        winter_l = jnp.where(hm, w_inter, winter_l)
        emt_l = jnp.where(hm, jnp.exp(-mt), emt_l)
        bl = bc[lc - 1:lc, :]
        glog = bl - bc + igc
        mn = jnp.maximum(bl + mp, jnp.max(glog, axis=0, keepdims=True))
        ws_l = jnp.where(hm, jnp.exp(glog - mn), ws_l)
        a_l = jnp.where(hm, jnp.exp(bl + mp - mn), a_l)
        m_new_row = jnp.where(lane_s == h, mn, m_new_row)

    num = winter_l * qc + num_intra
    hh = num / jnp.maximum(jnp.abs(den_l), emt_l)
    hsq = hh * hh
    ms_l = zeros_l
    for h in range(M_HEADS):
        hm = (lane >= h * M_DV) & (lane < (h + 1) * M_DV)
        ms = jnp.sum(jnp.where(hm, hsq, 0.0), axis=1, keepdims=True) * (1.0 / M_DV)
        ms_l = jnp.where(hm, ms, ms_l)
    hn = hh * lax.rsqrt(ms_l + EPS)
    a_out = hn * norm_w * jax.nn.sigmoid(zo)

    kw = k * ws_l
    upd = lax.dot_general(kw.astype(BF16), vb, (((0,), (0,)), ((), ())), preferred_element_type=F32)
    r2 = lax.broadcasted_iota(jnp.int32, (w, w), 0)
    c2 = lax.broadcasted_iota(jnp.int32, (w, w), 1)
    same_head = (r2 // M_DK) == (c2 // M_DV)
    c_new = a_l * cmat + jnp.where(same_head, upd, 0.0)
    n_new = a_l * nrow + jnp.sum(kw, axis=0, keepdims=True)
    return a_out, c_new, n_new, m_new_row


def _mlstm_call(za3, small3, gates_t, c0, n0, m0, norm_w, g_sz, lc, act_dtype):
    nseq, t, _ = za3.shape
    w = M_WIDTH
    col = lambda b: pl.BlockSpec((g_sz, lc, w), lambda i, j: (i, j, b))
    st_c = pl.BlockSpec((g_sz, w, w), lambda i, j: (i, 0, 0))
    st_n = pl.BlockSpec((g_sz, 1, w), lambda i, j: (i, 0, 0))
    st_m = pl.BlockSpec((g_sz, 1, LANES), lambda i, j: (i, 0, 0))
    return pl.pallas_call(
        _mlstm_kernel,
        out_shape=(jax.ShapeDtypeStruct((nseq, t, w), act_dtype),
                   jax.ShapeDtypeStruct((nseq, w, w), F32),
                   jax.ShapeDtypeStruct((nseq, 1, w), F32),
                   jax.ShapeDtypeStruct((nseq, 1, LANES), F32)),
        grid=(nseq // g_sz, t // lc),
        in_specs=[col(3), col(4), col(5), col(6),
                  pl.BlockSpec((g_sz, lc, LANES), lambda i, j: (i, j, 0)),
                  pl.BlockSpec((g_sz, 2 * M_HEADS, lc), lambda i, j: (i, 0, j)),
                  st_c, st_n, st_m,
                  pl.BlockSpec((1, w), lambda i, j: (0, 0))],
        out_specs=(pl.BlockSpec((g_sz, lc, w), lambda i, j: (i, j, 0)), st_c, st_n, st_m),
        scratch_shapes=[pltpu.VMEM((g_sz, w, w), F32), pltpu.VMEM((g_sz, 1, w), F32),
                        pltpu.VMEM((g_sz, 1, LANES), F32)],
        compiler_params=_cparams(2),
        name="mlstm",
    )(za3, za3, za3, za3, small3, gates_t, c0, n0, m0, norm_w)


def _qprep_kernel(cq_ref, wq_ref, wqs_ref, wuk_ref, cos_ref, sin_ref, q_ref):
    g_sz, _, tt, _ = q_ref.shape
    cq = cq_ref[...]
    qa = _bdot(cq, wq_ref[...])
    qs = _bdot(cq, wqs_ref[...])
    cos = cos_ref[...]
    sin = sin_ref[...]
    lane = lax.broadcasted_iota(jnp.int32, cos.shape, 1)
    is_rope = (lane >= SM_KR) & (lane < SM_KR + QK_ROPE)
    scale = (QK_NOPE + QK_ROPE) ** -0.5 * LOG2E
    for h in range(A_HEADS):
        slot = qa[:, h * Q_SLOT:(h + 1) * Q_SLOT] * cos + qs[:, h * Q_SLOT:(h + 1) * Q_SLOT] * sin
        q_lat = _bdot(slot.astype(BF16), wuk_ref[h]) * scale
        q_rope = jnp.where(is_rope, slot, 0.0) * scale
        q_ref[:, h, :, 0:KV_LORA] = q_lat.reshape(g_sz, tt, KV_LORA).astype(q_ref.dtype)
        q_ref[:, h, :, KV_LORA:QK_CAT] = q_rope.reshape(g_sz, tt, LANES).astype(q_ref.dtype)


def _qprep_call(cq, wq, wqs, wuk, cos_t, sin_t, layer, nseq, t, g_sz, tt, q_dtype):
    m = g_sz * tt
    nj = t // tt
    qw = A_HEADS * Q_SLOT
    return pl.pallas_call(
        _qprep_kernel,
        out_shape=jax.ShapeDtypeStruct((nseq, A_HEADS, t, QK_CAT), q_dtype),
        grid=(nseq // g_sz, nj),
        in_specs=[
            pl.BlockSpec((m, Q_LORA), lambda i, j: (i * nj + j, 0)),
            pl.BlockSpec((None, Q_LORA, qw), lambda i, j: (layer, 0, 0)),
            pl.BlockSpec((None, Q_LORA, qw), lambda i, j: (layer, 0, 0)),
            pl.BlockSpec((None, A_HEADS, Q_SLOT, KV_LORA), lambda i, j: (layer, 0, 0, 0)),
            pl.BlockSpec((m, LANES), lambda i, j: (j, 0)),
            pl.BlockSpec((m, LANES), lambda i, j: (j, 0)),
        ],
        out_specs=pl.BlockSpec((g_sz, A_HEADS, tt, QK_CAT), lambda i, j: (i, 0, j, 0)),
        compiler_params=_cparams(2),
        name="mla_qprep",
    )(cq, wq, wqs, wuk, cos_t, sin_t)


def _fold_lanes(x, op):
    out = x[:, 0:LANES]
    for j in range(1, x.shape[1] // LANES):
        out = op(out, x[:, j * LANES:(j + 1) * LANES])
    return out


def _attn_prompt_kernel(q_ref, k_ref, o_ref, s_ref, p_ref, m_ref, al_ref, l_ref, acc_ref, *, tk):
    qi = pl.program_id(1)
    tq = q_ref.shape[2]
    rows = A_HEADS * tq
    half = rows // 2
    hh = A_HEADS // 2
    rc = ATT_ROW_CHUNK
    n_lane_groups = tk // LANES
    m_ref[...] = jnp.full_like(m_ref, NEG_INF)
    l_ref[...] = jnp.zeros_like(l_ref)
    acc_ref[...] = jnp.zeros_like(acc_ref)
    col_minus_row = (lax.broadcasted_iota(jnp.int32, (rc, tk), 1)
                     - lax.broadcasted_iota(jnp.int32, (rc, tk), 0))

    def block(kb, masked, n_groups):
        nk = n_groups * LANES
        k = k_ref[pl.ds(pl.multiple_of(kb * tk, tk), nk), :]
        for b in range(2):
            qh = q_ref[0, b * hh:(b + 1) * hh].reshape(half, QK_CAT)
            s_ref[b * half:(b + 1) * half, 0:nk] = _bdot_nt(qh, k)
        for c in range(rows // rc):
            r = slice(c * rc, (c + 1) * rc)
            sc = s_ref[r, 0:nk]
            if masked:
                thr = qi * tq - kb * tk + (c * rc) % tq
                sc = jnp.where(col_minus_row[:, 0:nk] <= thr, sc, NEG_INF)
                s_ref[r, 0:nk] = sc
            m_prev = m_ref[r, :]
            m_new = jnp.maximum(m_prev, jnp.max(_fold_lanes(sc, jnp.maximum), axis=1, keepdims=True))
            al_ref[r, :] = jnp.exp2(m_prev - m_new)
            m_ref[r, :] = m_new
        for c in range(rows // rc):
            r = slice(c * rc, (c + 1) * rc)
            m_new = m_ref[r, :]
            part = None
            for j in range(n_groups):
                cols = slice(j * LANES, (j + 1) * LANES)
                p = jnp.exp2(s_ref[r, cols] - m_new)
                p_ref[r, cols] = p.astype(BF16)
                part = p if part is None else part + p
            l_ref[r, :] = al_ref[r, :] * l_ref[r, :] + part
        kv = k[:, 0:KV_LORA]
        for b in range(2):
            r = slice(b * half, (b + 1) * half)
            pv = _bdot(p_ref[r, 0:nk], kv)
            alpha = al_ref[r, :]
            for j in range(KV_LORA // LANES):
                cols = slice(j * LANES, (j + 1) * LANES)
                acc_ref[r, cols] = alpha * acc_ref[r, cols] + pv[:, cols]

    n_kb = (qi * tq + tq + tk - 1) // tk

    def body(kb, carry):
        block(kb, False, n_lane_groups)
        return carry

    lax.fori_loop(0, n_kb - 1, body, 0)
    tiles_per_block = tk // tq
    for v in range(tiles_per_block):
        @pl.when(qi % tiles_per_block == v)
        def _(v=v):
            block(n_kb - 1, True, ((v + 1) * tq + LANES - 1) // LANES)

    l_row = jnp.sum(l_ref[...], axis=1, keepdims=True)
    o_ref[0] = (acc_ref[...] / l_row).reshape(A_HEADS, tq, KV_LORA).astype(o_ref.dtype)


def _attn_prompt_call(q, kcat3):
    nseq, _, t, _ = q.shape
    tq = min(ATT_TQ, t)
    tk = min(ATT_TK, t)
    assert tk % tq == 0 and tq % ATT_ROW_CHUNK == 0 and t % tk == 0
    rows = A_HEADS * tq
    return pl.pallas_call(
        functools.partial(_attn_prompt_kernel, tk=tk),
        out_shape=jax.ShapeDtypeStruct((nseq, A_HEADS, t, KV_LORA), BF16),
        grid=(nseq, t // tq),
        in_specs=[pl.BlockSpec((1, A_HEADS, tq, QK_CAT), lambda i, j: (i, 0, j, 0)),
                  pl.BlockSpec((None, t, QK_CAT), lambda i, j: (i, 0, 0))],
        out_specs=pl.BlockSpec((1, A_HEADS, tq, KV_LORA), lambda i, j: (i, 0, j, 0)),
        scratch_shapes=[pltpu.VMEM((rows, tk), F32), pltpu.VMEM((rows, tk), BF16),
                        pltpu.VMEM((rows, LANES), F32), pltpu.VMEM((rows, LANES), F32),
                        pltpu.VMEM((rows, LANES), F32), pltpu.VMEM((rows, KV_LORA), F32)],
        compiler_params=_cparams(2),
        name="attn_prompt",
    )(q, kcat3)


def _attn_sample_kernel(pt_ref, q_ref, knew_ref, lat_hbm, krt_hbm, o_ref, latbuf, krtbuf, kbuf, krbuf, sems,
                        *, layer, n_pages):
    step = pl.program_id(0)
    n_seq = pl.num_programs(0)
    slot = step % 2
    nxt = jnp.minimum(step + 1, n_seq - 1)
    dec = q_ref.shape[2]
    rows = A_HEADS * dec

    def page_copies(seq, slot_):
        copies = []
        for i in range(n_pages):
            page = 0 if seq is None else pt_ref[seq, i]
            keys = pl.ds(i * PAGE_SIZE, PAGE_SIZE)
            copies.append(pltpu.make_async_copy(lat_hbm.at[layer, page], latbuf.at[slot_, keys, :],
                                                sems.at[0, slot_]))
            copies.append(pltpu.make_async_copy(krt_hbm.at[layer, page], krtbuf.at[slot_, :, keys],
                                                sems.at[1, slot_]))
        return copies

    @pl.when(step == 0)
    def _():
        for c in page_copies(0, 0):
            c.start()

    for c in page_copies(nxt, 1 - slot):
        c.start()
    for c in page_copies(None, slot):
        c.wait()

    for i in range(n_pages):
        keys = slice(i * PAGE_SIZE, (i + 1) * PAGE_SIZE)
        kbuf[keys, :] = latbuf[slot, keys, :].astype(BF16)
        krbuf[:, keys] = krtbuf[slot, :, keys].astype(BF16)
    q = q_ref[0].reshape(rows, QK_CAT)
    q_lat = q[:, 0:KV_LORA].astype(BF16)
    q_rope = q[:, KV_LORA + SM_KR:KV_LORA + SM_KR + QK_ROPE].astype(BF16)
    kv = kbuf[...]
    s = _bdot_nt(q_lat, kv) + _bdot(q_rope, krbuf[...])
    knew = knew_ref[0].astype(BF16)
    s2 = _bdot_nt(q.astype(BF16), knew)
    row = lax.broadcasted_iota(jnp.int32, s2.shape, 0)
    col = lax.broadcasted_iota(jnp.int32, s2.shape, 1)
    s2 = jnp.where(col <= (row & (dec - 1)), s2, NEG_INF)
    m = jnp.maximum(jnp.max(s, axis=1, keepdims=True), jnp.max(s2, axis=1, keepdims=True))
    p = jnp.exp2(s - m)
    p2 = jnp.exp2(s2 - m)
    l = jnp.sum(p, axis=1, keepdims=True) + jnp.sum(p2, axis=1, keepdims=True)
    o = _bdot(p.astype(BF16), kv) + _bdot(p2.astype(BF16), knew[:, 0:KV_LORA])
    o_ref[0] = (o / l).reshape(A_HEADS, dec, KV_LORA)

    @pl.when(step == n_seq - 1)
    def _():
        for c in page_copies(None, 1 - slot):
            c.wait()


def _attn_sample_call(page_table, q, kcat3, cache_latent, cache_krope_t, layer):
    nseq, _, dec, _ = q.shape
    n_pages = page_table.shape[1]
    n_keys = n_pages * PAGE_SIZE
    assert dec & (dec - 1) == 0
    grid_spec = pltpu.PrefetchScalarGridSpec(
        num_scalar_prefetch=1,
        grid=(nseq,),
        in_specs=[pl.BlockSpec((1, A_HEADS, dec, QK_CAT), lambda s, pt: (s, 0, 0, 0)),
                  pl.BlockSpec((1, dec, QK_CAT), lambda s, pt: (s, 0, 0)),
                  pl.BlockSpec(memory_space=pl.ANY),
                  pl.BlockSpec(memory_space=pl.ANY)],
        out_specs=pl.BlockSpec((1, A_HEADS, dec, KV_LORA), lambda s, pt: (s, 0, 0, 0)),
        scratch_shapes=[pltpu.VMEM((2, n_keys, KV_LORA), F32),
                        pltpu.VMEM((2, QK_ROPE, n_keys), F32),
                        pltpu.VMEM((n_keys, KV_LORA), BF16),
                        pltpu.VMEM((QK_ROPE, n_keys), BF16),
                        pltpu.SemaphoreType.DMA((2, 2))],
    )
    return pl.pallas_call(
        functools.partial(_attn_sample_kernel, layer=layer, n_pages=n_pages),
        out_shape=jax.ShapeDtypeStruct((nseq, A_HEADS, dec, KV_LORA), F32),
        grid_spec=grid_spec,
        compiler_params=_cparams(1),
        name="attn_sample",
    )(page_table, q, kcat3, cache_latent, cache_krope_t)


def _merge_kernel(h_ref, g2_ref, ac_ref, am_ref, ol_ref, gates_ref, wc_ref, wm_ref, wuv_ref, wo_ref, wmix_ref,
                  o_ref):
    g_sz, tt, _ = h_ref.shape
    m = g_sz * tt
    y_conv = _bdot(ac_ref[...].astype(BF16), wc_ref[...])
    y_m = _bdot(am_ref[...].astype(BF16), wm_ref[...])
    parts = []
    for h in range(A_HEADS):
        ol = ol_ref[:, h].reshape(m, KV_LORA).astype(BF16)
        parts.append(_bdot(ol, wuv_ref[h]))
    o = jnp.concatenate(parts, axis=-1).astype(BF16)
    y_a = _bdot(o, wo_ref[...])
    gate = lambda b: gates_ref[:, b * D_MODEL:(b + 1) * D_MODEL].astype(F32)
    merged = gate(0) * y_conv + gate(1) * y_m + gate(2) * y_a
    mixed = _bdot(merged.astype(BF16), wmix_ref[...])
    o_ref[...] = h_ref[...] + g2_ref[...] * mixed.reshape(g_sz, tt, D_MODEL)


def _merge_call(h, mods, layer, a_conv, a_m, o_lat, gates, wc, wm, wuv, wo, wmix, g_sz, tt):
    nseq, t, _ = h.shape
    m = g_sz * tt
    nj = t // tt
    rows = lambda w: pl.BlockSpec((m, w), lambda i, j: (i * nj + j, 0))
    row3 = pl.BlockSpec((g_sz, tt, D_MODEL), lambda i, j: (i, j, 0))
    wspec = lambda a, b: pl.BlockSpec((None, a, b), lambda i, j: (layer, 0, 0))
    return pl.pallas_call(
        _merge_kernel,
        out_shape=jax.ShapeDtypeStruct(h.shape, F32),
        grid=(nseq // g_sz, nj),
        in_specs=[
            row3,
            _mod_spec(layer, 5, g_sz),
            rows(CONV_WIDTH), rows(M_WIDTH),
            pl.BlockSpec((g_sz, A_HEADS, tt, KV_LORA), lambda i, j: (i, 0, j, 0)),
            rows(3 * D_MODEL),
            wspec(CONV_WIDTH, D_MODEL), wspec(M_WIDTH, D_MODEL),
            pl.BlockSpec((None, A_HEADS, KV_LORA, V_HEAD), lambda i, j: (layer, 0, 0, 0)),
            wspec(A_HEADS * V_HEAD, D_MODEL), wspec(D_MODEL, D_MODEL),
        ],
        out_specs=row3,
        compiler_params=_cparams(2),
        name="merge",
    )(h, mods, a_conv, a_m, o_lat, gates, wc, wm, wuv, wo, wmix)


def _rope_tables(pos):
    half = QK_ROPE // 2
    inv = ROPE_BASE ** (-jnp.arange(half, dtype=F32) * 2.0 / QK_ROPE)
    ang = pos.astype(F32)[:, None] * inv[None, :]
    cos, sin = jnp.cos(ang), jnp.sin(ang)
    n = pos.shape[0]
    ones = jnp.ones((n, SM_KR), F32)
    tail = LANES - SM_KR - QK_ROPE
    cos_t = jnp.concatenate([ones, cos, cos, jnp.ones((n, tail), F32)], axis=1)
    sin_t = jnp.concatenate([jnp.zeros((n, SM_KR), F32), -sin, sin, jnp.zeros((n, tail), F32)], axis=1)
    return cos_t, sin_t


def _rearranged_w_in(w_in):
    depth = w_in.shape[0]
    off = np.cumsum((0,) + (CONV_WIDTH,) * 3 + (M_WIDTH,) * 4 + (M_HEADS,) * 2
                    + (Q_LORA, KV_LORA, QK_ROPE) + (D_MODEL,) * 3)
    zi = w_in[:, :, off[7]:off[8]]
    zf = w_in[:, :, off[8]:off[9]]
    zdq = w_in[:, :, off[9]:off[10]]
    zdkv = w_in[:, :, off[10]:off[11]]
    zkr = w_in[:, :, off[11]:off[12]]
    gates = w_in[:, :, off[12]:off[15]]
    zeros = lambda n: jnp.zeros((depth, D_MODEL, n), w_in.dtype)
    half = QK_ROPE // 2
    small_a = jnp.concatenate([zi, zf, zeros(SM_KR - 2 * M_HEADS), zkr, zeros(LANES - SM_KR - QK_ROPE)], axis=2)
    small_b = jnp.concatenate([zeros(SM_KR), zkr[:, :, half:], zkr[:, :, :half],
                               zeros(LANES - SM_KR - QK_ROPE)], axis=2)
    return jnp.concatenate([w_in[:, :, 0:off[7]], zdq, zdkv, gates, small_a, small_b], axis=2).astype(BF16)


def _rearranged_w_uq(w_uq):
    depth = w_uq.shape[0]
    per = QK_NOPE + QK_ROPE
    half = QK_ROPE // 2
    w = w_uq.reshape(depth, Q_LORA, A_HEADS, per)
    nope = w[..., :QK_NOPE]
    r1 = w[..., QK_NOPE:QK_NOPE + half]
    r2 = w[..., QK_NOPE + half:]
    pad = jnp.zeros((depth, Q_LORA, A_HEADS, Q_SLOT - per), w_uq.dtype)
    wq = jnp.concatenate([nope, r1, r2, pad], axis=-1)
    wqs = jnp.concatenate([jnp.zeros_like(nope), r2, r1, pad], axis=-1)
    flat = lambda a: a.reshape(depth, Q_LORA, A_HEADS * Q_SLOT).astype(BF16)
    return flat(wq), flat(wqs)


def _block_diag_state(c):
    nseq = c.shape[0]
    eye = jnp.eye(M_HEADS, dtype=c.dtype)
    return jnp.einsum('bhde,hg->bhdge', c, eye).reshape(nseq, M_HEADS * M_DK, M_HEADS * M_DV)


def _diag_blocks(c_bd):
    nseq = c_bd.shape[0]
    c5 = c_bd.reshape(nseq, M_HEADS, M_DK, M_HEADS, M_DV)
    return jnp.stack([c5[:, h, :, h, :] for h in range(M_HEADS)], axis=1)


def _tile_plan(nseq, t):
    if t >= MLSTM_CHUNK_P:
        blk = lambda rows: (1, min(rows, t))
        act = BF16
    else:
        blk = lambda rows: (min(max(rows // t, 1), nseq), t)
        act = F32
    plan = dict(ffn=blk(1024), inproj=blk(256), conv=blk(1024 if act == F32 else 512), qprep=blk(512),
                merge=blk(512), mlstm=(min(MLSTM_SEQS_P if act == BF16 else MLSTM_SEQS_S, nseq), min(MLSTM_CHUNK_P, t)),
                act=act)
    for name in ('ffn', 'inproj', 'conv', 'qprep', 'merge', 'mlstm'):
        g_sz, tt = plan[name]
        assert nseq % g_sz == 0 and t % tt == 0, (name, nseq, t)
    return plan


def _layer_group(x, mods, layer, wts, conv_prev, c0, n0, m0, cos_t, sin_t, attend, final):
    nseq, t, _ = x.shape
    plan = _tile_plan(nseq, t)
    act_dtype = plan['act']
    h = _ffn_call(x, mods, layer, True, wts['ln_ffn1'], wts['ffn1_w13'], wts['ffn1_w2'], wts['final_norm'],
                  False, *plan['ffn'])
    za, cq, ckv, kcat, gates, small = _inproj_call(
        h, mods, layer, wts['ln_mix'], wts['w_in_r'], wts['mla_q_norm'], wts['mla_kv_norm'],
        cos_t, sin_t, wts['brow'], *plan['inproj'], act_dtype)
    za3 = za.reshape(nseq, t, ZA_W)
    small3 = small.reshape(nseq, t, LANES)
    a_conv, conv_new = _conv_call(za3, conv_prev, wts['conv_w'][layer], *plan['conv'])
    gates_t = jnp.swapaxes(small3[:, :, 0:2 * M_HEADS], 1, 2)
    a_m, c_bd, n1, m1 = _mlstm_call(za3, small3, gates_t, c0, n0, m0, wts['mlstm_norm'][layer], *plan['mlstm'],
                                    act_dtype)
    a_m = a_m.reshape(nseq * t, M_WIDTH)
    q = _qprep_call(cq, wts['wq'], wts['wqs'], wts['wuk'], cos_t, sin_t, layer, nseq, t, *plan['qprep'], act_dtype)
    o_lat = attend(q, kcat.reshape(nseq, t, QK_CAT))
    h2 = _merge_call(h, mods, layer, a_conv, a_m, o_lat, gates, wts['conv_out'], wts['mlstm_out'], wts['wuv'],
                     wts['mla_w_o'], wts['w_mix_out'], *plan['merge'])
    out = _ffn_call(h2, mods, layer, False, wts['ln_ffn2'], wts['ffn2_w13'], wts['ffn2_w2'], wts['final_norm'],
                    final, *plan['ffn'])
    state = (ckv.reshape(nseq, t, KV_LORA), small3[:, :, SM_KR:SM_KR + QK_ROPE], conv_new,
             _diag_blocks(c_bd), n1.reshape(nseq, M_HEADS, M_DK), m1[:, 0, 0:M_HEADS])
    return out, state


def kernel(x_prompt, x_sample, cache_latent, cache_krope, state_conv, state_mlstm_C, state_mlstm_n, state_mlstm_m, page_table, c_prompt, c_sample, ln_ffn1, ffn1_w13, ffn1_w2, ln_mix, w_in, conv_w, conv_out, mlstm_b_i, mlstm_b_f, mlstm_norm, mlstm_out, mla_q_norm, mla_w_uq, mla_w_uk, mla_kv_norm, mla_w_uv, mla_w_o, w_mix_out, ln_ffn2, ffn2_w13, ffn2_w2, w_ada, b_ada, final_norm):
    bp, seq, _ = x_prompt.shape
    bs, dec, _ = x_sample.shape
    past_len = page_table.shape[1] * PAGE_SIZE

    c_all = jnp.concatenate([c_prompt, c_sample], axis=0)
    mods = _mods_call(c_all, w_ada, b_ada)
    mods_p = mods[:, :, :bp].reshape(DEPTH, 9, bp, 1, D_MODEL)
    mods_s = mods[:, :, bp:].reshape(DEPTH, 9, bs, 1, D_MODEL)

    wq, wqs = _rearranged_w_uq(mla_w_uq)
    wuk = mla_w_uk.reshape(DEPTH, KV_LORA, A_HEADS, QK_NOPE).transpose(0, 2, 3, 1)
    wuk = jnp.pad(wuk, ((0, 0), (0, 0), (0, Q_SLOT - QK_NOPE), (0, 0))).astype(BF16)
    brow = jnp.concatenate([mlstm_b_i, mlstm_b_f, jnp.zeros((DEPTH, LANES - 2 * M_HEADS), F32)], axis=1)
    wts = {
        'ln_ffn1': ln_ffn1.reshape(DEPTH, 1, D_MODEL), 'ffn1_w13': ffn1_w13.astype(BF16), 'ffn1_w2': ffn1_w2.astype(BF16),
        'ln_mix': ln_mix.reshape(DEPTH, 1, D_MODEL), 'w_in_r': _rearranged_w_in(w_in), 'conv_w': conv_w,
        'conv_out': conv_out.astype(BF16), 'brow': brow.reshape(DEPTH, 1, LANES),
        'mlstm_norm': mlstm_norm.reshape(DEPTH, 1, M_WIDTH), 'mlstm_out': mlstm_out.astype(BF16),
        'mla_q_norm': mla_q_norm.reshape(DEPTH, 1, Q_LORA), 'wq': wq, 'wqs': wqs, 'wuk': wuk, 'mla_kv_norm': mla_kv_norm.reshape(DEPTH, 1, KV_LORA),
        'wuv': mla_w_uv.reshape(DEPTH, KV_LORA, A_HEADS, V_HEAD).transpose(0, 2, 1, 3).astype(BF16),
        'mla_w_o': mla_w_o.astype(BF16), 'w_mix_out': w_mix_out.astype(BF16),
        'ln_ffn2': ln_ffn2.reshape(DEPTH, 1, D_MODEL), 'ffn2_w13': ffn2_w13.astype(BF16), 'ffn2_w2': ffn2_w2.astype(BF16),
        'final_norm': final_norm.reshape(1, D_MODEL),
    }

    cos_p, sin_p = _rope_tables(jnp.arange(seq))
    cos_s, sin_s = _rope_tables(past_len + jnp.arange(dec))
    cos_s = jnp.tile(cos_s, (bs, 1))
    sin_s = jnp.tile(sin_s, (bs, 1))

    cache_krope_t = jnp.swapaxes(cache_krope, 2, 3)

    zero_conv = jnp.zeros((bp, CONV_K - 1, CONV_WIDTH), F32)
    zero_c = jnp.zeros((bp, M_WIDTH, M_WIDTH), F32)
    zero_n = jnp.zeros((bp, 1, M_WIDTH), F32)
    zero_m = jnp.zeros((bp, 1, LANES), F32)

    xp, xs = x_prompt, x_sample
    p_st, s_st = [], []
    for layer in range(DEPTH):
        final = layer == DEPTH - 1
        xp, sp = _layer_group(xp, mods_p, layer, wts, zero_conv, zero_c, zero_n, zero_m,
                              cos_p, sin_p, _attn_prompt_call, final)
        p_st.append(sp)
        m0 = jnp.pad(state_mlstm_m[layer], ((0, 0), (0, LANES - M_HEADS))).reshape(bs, 1, LANES)
        attend_s = functools.partial(_attn_sample_call, page_table, cache_latent=cache_latent,
                                     cache_krope_t=cache_krope_t, layer=layer)
        xs, ss = _layer_group(xs, mods_s, layer, wts, state_conv[layer],
                              _block_diag_state(state_mlstm_C[layer]),
                              state_mlstm_n[layer].reshape(bs, 1, M_WIDTH), m0,
                              cos_s, sin_s, attend_s, final)
        s_st.append(ss)
    p_out = tuple(jnp.stack(a) for a in zip(*p_st))
    s_out = tuple(jnp.stack(a) for a in zip(*s_st))
    return (xp, xs) + p_out + s_out
```

```python
import functools

import jax
import jax.numpy as jnp
import numpy as np
from jax import lax
from jax.experimental import pallas as pl
from jax.experimental.pallas import tpu as pltpu

D_MODEL = 1024
DEPTH = 4
PAGE_SIZE = 128
D_FF = 2816
CONV_WIDTH = 256
CONV_K = 3
M_HEADS = 4
M_DK = 64
M_DV = 64
M_WIDTH = M_HEADS * M_DV
A_HEADS = 8
QK_NOPE = 64
QK_ROPE = 32
V_HEAD = 64
Q_LORA = 384
KV_LORA = 256
ROPE_BASE = 10000.0
EPS = 1e-6

LANES = 128
VMEM_LIMIT = 56 * 1024 * 1024

F32 = jnp.float32
BF16 = jnp.bfloat16
NEG_INF = float("-inf")
LOG2E = 1.4426950408889634

ZA_W = 7 * 256
OFF_DQ = ZA_W
OFF_DKV = OFF_DQ + Q_LORA
OFF_GATES = OFF_DKV + KV_LORA
OFF_SMALL_A = OFF_GATES + 3 * D_MODEL
OFF_SMALL_B = OFF_SMALL_A + LANES
N_IN_R = OFF_SMALL_B + LANES
SM_IG = 0
SM_LF = M_HEADS
SM_KR = 64
QK_CAT = KV_LORA + LANES
Q_SLOT = LANES

FFN_CHUNKS = (768, 768, 768, 512)
assert sum(FFN_CHUNKS) == D_FF
ATT_TQ = 256
ATT_TK = 512
ATT_ROW_CHUNK = 32
ATT_PARTS = 8
MLSTM_CHUNK_P = 256
MLSTM_SEQS_P = 1
MLSTM_SEQS_S = 8


def _cparams(n_axes):
    return pltpu.CompilerParams(dimension_semantics=("arbitrary",) * n_axes,
                                vmem_limit_bytes=VMEM_LIMIT)


def _bdot(a, b):
    return jnp.dot(a, b, preferred_element_type=F32)


def _bdot_nt(a, b):
    return lax.dot_general(a, b, (((1,), (1,)), ((), ())), preferred_element_type=F32)


def _rms(x):
    return x * lax.rsqrt(jnp.mean(x * x, axis=-1, keepdims=True) + EPS)


def _mods_kernel(c_ref, w_ref, b_ref, o_ref):
    c = c_ref[...]
    a = (c * jax.nn.sigmoid(c)).astype(BF16)
    o_ref[...] = _bdot(a, w_ref[...].astype(BF16)) + b_ref[...]


def _mods_call(c_all, w_ada, b_ada):
    nb = c_all.shape[0]
    return pl.pallas_call(
        _mods_kernel,
        out_shape=jax.ShapeDtypeStruct((DEPTH, 9, nb, D_MODEL), F32),
        grid=(DEPTH, 9),
        in_specs=[
            pl.BlockSpec((nb, D_MODEL), lambda l, j: (0, 0)),
            pl.BlockSpec((None, D_MODEL, D_MODEL), lambda l, j: (l, 0, j)),
            pl.BlockSpec((None, 1, D_MODEL), lambda l, j: (l, 0, j)),
        ],
        out_specs=pl.BlockSpec((None, None, nb, D_MODEL), lambda l, j: (l, j, 0, 0)),
        compiler_params=_cparams(2),
        name="adaln_mods",
    )(c_all, w_ada, b_ada.reshape(DEPTH, 1, 9 * D_MODEL))


def _vec_spec(layer, width):
    return pl.BlockSpec((None, 1, width), lambda *_: (layer, 0, 0))


def _mod_spec(layer, which, g):
    return pl.BlockSpec((None, None, g, 1, D_MODEL), lambda i, *_: (layer, which, i, 0, 0))


def _ffn_kernel(x_ref, ln_ref, sh_ref, sc_ref, g_ref, w13_ref, w2_ref, fn_ref, o_ref, *, final):
    g_sz, tt, _ = x_ref.shape
    x = x_ref[...]
    u = (_rms(x) * ln_ref[...]) * (1 + sc_ref[...]) + sh_ref[...]
    xn = u.reshape(g_sz * tt, D_MODEL).astype(BF16)
    y = None
    start = 0
    for width in FFN_CHUNKS:
        cols = slice(start, start + width)
        gate = _bdot(xn, w13_ref[:, cols])
        up = _bdot(xn, w13_ref[:, D_FF + start:D_FF + start + width])
        start += width
        act = (gate * jax.nn.sigmoid(gate) * up).astype(BF16)
        part = _bdot(act, w2_ref[cols, :])
        y = part if y is None else y + part
    out = x + (0.5 * g_ref[...]) * y.reshape(g_sz, tt, D_MODEL)
    if final:
        out = _rms(out) * fn_ref[...]
    o_ref[...] = out


def _ffn_call(x, mods, layer, first, ln, w13, w2, fnorm, final, g_sz, tt):
    nseq, t, _ = x.shape
    base = 0 if first else 6
    row = pl.BlockSpec((g_sz, tt, D_MODEL), lambda i, j: (i, j, 0))
    resident = lambda a, b: pl.BlockSpec((None, a, b), lambda i, j: (layer, 0, 0), pipeline_mode=pl.Buffered(1))
    return pl.pallas_call(
        functools.partial(_ffn_kernel, final=final),
        out_shape=jax.ShapeDtypeStruct(x.shape, F32),
        grid=(nseq // g_sz, t // tt),
        in_specs=[
            row,
            _vec_spec(layer, D_MODEL),
            _mod_spec(layer, base + 0, g_sz),
            _mod_spec(layer, base + 1, g_sz),
            _mod_spec(layer, base + 2, g_sz),
            resident(D_MODEL, 2 * D_FF),
            resident(D_FF, D_MODEL),
            pl.BlockSpec((1, D_MODEL), lambda i, j: (0, 0)),
        ],
        out_specs=row,
        compiler_params=_cparams(2),
        name="ffn",
    )(x, ln, mods, mods, mods, w13, w2, fnorm)


def _inproj_kernel(h_ref, ln_ref, sh_ref, sc_ref, w_ref, qn_ref, kvn_ref, cos_ref, sin_ref, brow_ref,
                   za_ref, cq_ref, ckv_ref, kcat_ref, gates_ref, small_ref):
    g_sz, tt, _ = h_ref.shape
    m = g_sz * tt
    y = _rms(h_ref[...]) * ln_ref[...]
    u = y * (1 + sc_ref[...]) + sh_ref[...]
    xn = u.reshape(m, D_MODEL).astype(BF16)

    za_ref[...] = _bdot(xn, w_ref[:, 0:ZA_W])
    zdq = _bdot(xn, w_ref[:, OFF_DQ:OFF_DKV])
    cq_ref[...] = (_rms(zdq) * qn_ref[...]).astype(cq_ref.dtype)
    zdkv = _bdot(xn, w_ref[:, OFF_DKV:OFF_GATES])
    ckv = _rms(zdkv) * kvn_ref[...]
    ckv_ref[...] = ckv
    gates_ref[...] = jax.nn.sigmoid(_bdot(xn, w_ref[:, OFF_GATES:OFF_SMALL_A])).astype(gates_ref.dtype)

    sa = _bdot(xn, w_ref[:, OFF_SMALL_A:OFF_SMALL_B])
    sb = _bdot(xn, w_ref[:, OFF_SMALL_B:N_IN_R])
    rot = sa * cos_ref[...] + sb * sin_ref[...]
    lane = lax.broadcasted_iota(jnp.int32, rot.shape, 1)
    zb = rot + brow_ref[...]
    logsig = jnp.minimum(zb, 0.0) - jnp.log1p(jnp.exp(-jnp.abs(zb)))
    is_kr = (lane >= SM_KR) & (lane < SM_KR + QK_ROPE)
    small = jnp.where(lane < SM_LF, zb,
                      jnp.where(lane < SM_LF + M_HEADS, logsig,
                                jnp.where(is_kr, rot, 0.0)))
    small_ref[...] = small
    kcat_ref[:, 0:KV_LORA] = ckv.astype(kcat_ref.dtype)
    kcat_ref[:, KV_LORA:QK_CAT] = jnp.where(is_kr, rot, 0.0).astype(kcat_ref.dtype)


def _inproj_call(h, mods, layer, ln, w_in_r, qn, kvn, cos_t, sin_t, brow, g_sz, tt, act_dtype):
    nseq, t, _ = h.shape
    n = nseq * t
    m = g_sz * tt
    nj = t // tt
    grid = (nseq // g_sz, nj)
    rows = lambda w: pl.BlockSpec((m, w), lambda i, j: (i * nj + j, 0))
    out_shapes = (
        jax.ShapeDtypeStruct((n, ZA_W), F32),
        jax.ShapeDtypeStruct((n, Q_LORA), BF16),
        jax.ShapeDtypeStruct((n, KV_LORA), F32),
        jax.ShapeDtypeStruct((n, QK_CAT), act_dtype),
        jax.ShapeDtypeStruct((n, 3 * D_MODEL), BF16),
        jax.ShapeDtypeStruct((n, LANES), F32),
    )
    return pl.pallas_call(
        _inproj_kernel,
        out_shape=out_shapes,
        grid=grid,
        in_specs=[
            pl.BlockSpec((g_sz, tt, D_MODEL), lambda i, j: (i, j, 0)),
            _vec_spec(layer, D_MODEL),
            _mod_spec(layer, 3, g_sz),
            _mod_spec(layer, 4, g_sz),
            pl.BlockSpec((None, D_MODEL, N_IN_R), lambda i, j: (layer, 0, 0), pipeline_mode=pl.Buffered(1)),
            _vec_spec(layer, Q_LORA),
            _vec_spec(layer, KV_LORA),
            pl.BlockSpec((m, LANES), lambda i, j: (j, 0)),
            pl.BlockSpec((m, LANES), lambda i, j: (j, 0)),
            _vec_spec(layer, LANES),
        ],
        out_specs=(rows(ZA_W), rows(Q_LORA), rows(KV_LORA), rows(QK_CAT), rows(3 * D_MODEL), rows(LANES)),
        compiler_params=_cparams(2),
        name="inproj",
    )(h, ln, mods, mods, w_in_r, qn, kvn, cos_t, sin_t, brow)


def _conv_kernel(zb_ref, zc_ref, zx_ref, prev_ref, w_ref, a_ref, st_ref, carry_ref):
    j = pl.program_id(1)
    g_sz, tt, c = zc_ref.shape

    @pl.when(j == 0)
    def _():
        carry_ref[...] = prev_ref[...]

    cu = zc_ref[...] * zx_ref[...]
    prev = carry_ref[...]
    t = lax.broadcasted_iota(jnp.int32, cu.shape, 1)
    p0 = prev[:, 0:1, :]
    p1 = prev[:, 1:2, :]
    c1 = jnp.where(t == 0, p1, pltpu.roll(cu, 1, axis=1))
    c2 = jnp.where(t == 0, p0, jnp.where(t == 1, p1, pltpu.roll(cu, 2, axis=1)))
    w = w_ref[...]
    conv = w[0:1, :] * c2 + w[1:2, :] * c1 + w[2:3, :] * cu
    a = zb_ref[...] * conv
    a_ref[...] = a.reshape(g_sz * tt, c).astype(a_ref.dtype)
    new = cu[:, tt - (CONV_K - 1):tt, :]
    carry_ref[...] = new
    st_ref[...] = new


def _conv_call(za3, prev, conv_w_l, g_sz, tt):
    nseq, t, _ = za3.shape
    nj = t // tt
    c = CONV_WIDTH
    col = lambda b: pl.BlockSpec((g_sz, tt, c), lambda i, j: (i, j, b))
    st = pl.BlockSpec((g_sz, CONV_K - 1, c), lambda i, j: (i, 0, 0))
    return pl.pallas_call(
        _conv_kernel,
        out_shape=(jax.ShapeDtypeStruct((nseq * t, c), BF16),
                   jax.ShapeDtypeStruct((nseq, CONV_K - 1, c), F32)),
        grid=(nseq // g_sz, nj),
        in_specs=[col(0), col(1), col(2), st,
                  pl.BlockSpec((CONV_K, c), lambda i, j: (0, 0))],
        out_specs=(pl.BlockSpec((g_sz * tt, c), lambda i, j: (i * nj + j, 0)), st),
        scratch_shapes=[pltpu.VMEM((g_sz, CONV_K - 1, c), F32)],
        compiler_params=_cparams(2),
        name="conv_mixer",
    )(za3, za3, za3, prev, conv_w_l)


def _mlstm_kernel(q_ref, k_ref, v_ref, zo_ref, sm_ref, gt_ref, c0_ref, n0_ref, m0_ref, nw_ref,
                  a_ref, c_ref, n_ref, m_ref, c_s, n_s, m_s):
    j = pl.program_id(1)

    @pl.when(j == 0)
    def _():
        c_s[...] = c0_ref[...]
        n_s[...] = n0_ref[...]
        m_s[...] = m0_ref[...]

    for g in range(q_ref.shape[0]):
        a_out, c_new, n_new, m_new = _mlstm_chunk(
            q_ref[g], k_ref[g], v_ref[g], zo_ref[g], sm_ref[g], gt_ref[g], c_s[g], n_s[g], m_s[g], nw_ref[...])
        a_ref[g] = a_out.astype(a_ref.dtype)
        c_s[g] = c_new
        n_s[g] = n_new
        m_s[g] = m_new

    @pl.when(j == pl.num_programs(1) - 1)
    def _():
        c_ref[...] = c_s[...]
        n_ref[...] = n_s[...]
        m_ref[...] = m_s[...]


def _mlstm_chunk(q, k, v, zo, sm, gt, cmat, nrow, mrow, norm_w):
    lc = q.shape[0]
    w = M_WIDTH
    q = q * (M_DK ** -0.5)
    ri = lax.broadcasted_iota(jnp.int32, (lc, lc), 0)
    ci = lax.broadcasted_iota(jnp.int32, (lc, lc), 1)
    causal = ci <= ri
    bcol = jnp.dot(causal.astype(F32), sm, precision=lax.Precision.HIGHEST, preferred_element_type=F32)
    brow = jnp.dot(gt, (ri <= ci).astype(F32), precision=lax.Precision.HIGHEST, preferred_element_type=F32)

    lane = lax.broadcasted_iota(jnp.int32, (1, w), 1)
    lane_s = lax.broadcasted_iota(jnp.int32, (1, LANES), 1)
    kb = k.astype(BF16)
    vb = v.astype(BF16)
    qc = _bdot(q.astype(BF16), cmat.astype(BF16))

    zeros_l = jnp.zeros((lc, w), F32)
    num_intra = zeros_l
    den_l = zeros_l
    winter_l = zeros_l
    emt_l = zeros_l
    ws_l = zeros_l
    a_l = jnp.zeros((1, w), F32)
    m_new_row = mrow
    for h in range(M_HEADS):
        hm = (lane >= h * M_DV) & (lane < (h + 1) * M_DV)
        igc = sm[:, SM_IG + h:SM_IG + h + 1]
        bc = bcol[:, SM_LF + h:SM_LF + h + 1]
        igr = gt[SM_IG + h:SM_IG + h + 1, :]
        br = brow[SM_LF + h:SM_LF + h + 1, :]
        mp = mrow[:, h:h + 1]
        dlog = jnp.where(causal, bc - br + igr, NEG_INF)
        inter = bc + mp
        mt = jnp.maximum(inter, jnp.max(dlog, axis=1, keepdims=True))
        w_inter = jnp.exp(inter - mt)
        s = _bdot_nt(jnp.where(hm, q, 0.0).astype(BF16), kb) * jnp.exp(dlog - mt)
        num_intra = num_intra + _bdot(s.astype(BF16), jnp.where(hm, v, 0.0).astype(BF16))
        qn = jnp.sum(jnp.where(hm, q * nrow, 0.0), axis=1, keepdims=True)
        den = w_inter * qn + jnp.sum(s, axis=1, keepdims=True)
        den_l = jnp.where(hm, den, den_l)
        winter_l = jnp.where(hm, w_inter, winter_l)
        emt_l = jnp.where(hm, jnp.exp(-mt), emt_l)
        bl = bc[lc - 1:lc, :]
        glog = bl - bc + igc
        mn = jnp.maximum(bl + mp, jnp.max(glog, axis=0, keepdims=True))
        ws_l = jnp.where(hm, jnp.exp(glog - mn), ws_l)
        a_l = jnp.where(hm, jnp.exp(bl + mp - mn), a_l)
        m_new_row = jnp.where(lane_s == h, mn, m_new_row)

    num = winter_l * qc + num_intra
    hh = num / jnp.maximum(jnp.abs(den_l), emt_l)
    hsq = hh * hh
    ms_l = zeros_l
    for h in range(M_HEADS):
        hm = (lane >= h * M_DV) & (lane < (h + 1) * M_DV)
        ms = jnp.sum(jnp.where(hm, hsq, 0.0), axis=1, keepdims=True) * (1.0 / M_DV)
        ms_l = jnp.where(hm, ms, ms_l)
    hn = hh * lax.rsqrt(ms_l + EPS)
    a_out = hn * norm_w * jax.nn.sigmoid(zo)

    kw = k * ws_l
    upd = lax.dot_general(kw.astype(BF16), vb, (((0,), (0,)), ((), ())), preferred_element_type=F32)
    r2 = lax.broadcasted_iota(jnp.int32, (w, w), 0)
    c2 = lax.broadcasted_iota(jnp.int32, (w, w), 1)
    same_head = (r2 // M_DK) == (c2 // M_DV)
    c_new = a_l * cmat + jnp.where(same_head, upd, 0.0)
    n_new = a_l * nrow + jnp.sum(kw, axis=0, keepdims=True)
    return a_out, c_new, n_new, m_new_row


def _mlstm_call(za3, small3, gates_t, c0, n0, m0, norm_w, g_sz, lc, act_dtype):
    nseq, t, _ = za3.shape
    w = M_WIDTH
    col = lambda b: pl.BlockSpec((g_sz, lc, w), lambda i, j: (i, j, b))
    st_c = pl.BlockSpec((g_sz, w, w), lambda i, j: (i, 0, 0))
    st_n = pl.BlockSpec((g_sz, 1, w), lambda i, j: (i, 0, 0))
    st_m = pl.BlockSpec((g_sz, 1, LANES), lambda i, j: (i, 0, 0))
    return pl.pallas_call(
        _mlstm_kernel,
        out_shape=(jax.ShapeDtypeStruct((nseq, t, w), act_dtype),
                   jax.ShapeDtypeStruct((nseq, w, w), F32),
                   jax.ShapeDtypeStruct((nseq, 1, w), F32),
                   jax.ShapeDtypeStruct((nseq, 1, LANES), F32)),
        grid=(nseq // g_sz, t // lc),
        in_specs=[col(3), col(4), col(5), col(6),
                  pl.BlockSpec((g_sz, lc, LANES), lambda i, j: (i, j, 0)),
                  pl.BlockSpec((g_sz, 2 * M_HEADS, lc), lambda i, j: (i, 0, j)),
                  st_c, st_n, st_m,
                  pl.BlockSpec((1, w), lambda i, j: (0, 0))],
        out_specs=(pl.BlockSpec((g_sz, lc, w), lambda i, j: (i, j, 0)), st_c, st_n, st_m),
        scratch_shapes=[pltpu.VMEM((g_sz, w, w), F32), pltpu.VMEM((g_sz, 1, w), F32),
                        pltpu.VMEM((g_sz, 1, LANES), F32)],
        compiler_params=_cparams(2),
        name="mlstm",
    )(za3, za3, za3, za3, small3, gates_t, c0, n0, m0, norm_w)


def _qprep_kernel(cq_ref, wq_ref, wqs_ref, wuk_ref, cos_ref, sin_ref, q_ref):
    g_sz, _, tt, _ = q_ref.shape
    cq = cq_ref[...]
    qa = _bdot(cq, wq_ref[...])
    qs = _bdot(cq, wqs_ref[...])
    cos = cos_ref[...]
    sin = sin_ref[...]
    lane = lax.broadcasted_iota(jnp.int32, cos.shape, 1)
    is_rope = (lane >= SM_KR) & (lane < SM_KR + QK_ROPE)
    scale = (QK_NOPE + QK_ROPE) ** -0.5 * LOG2E
    for h in range(A_HEADS):
        slot = qa[:, h * Q_SLOT:(h + 1) * Q_SLOT] * cos + qs[:, h * Q_SLOT:(h + 1) * Q_SLOT] * sin
        q_lat = _bdot(slot.astype(BF16), wuk_ref[h]) * scale
        q_rope = jnp.where(is_rope, slot, 0.0) * scale
        q_ref[:, h, :, 0:KV_LORA] = q_lat.reshape(g_sz, tt, KV_LORA).astype(q_ref.dtype)
        q_ref[:, h, :, KV_LORA:QK_CAT] = q_rope.reshape(g_sz, tt, LANES).astype(q_ref.dtype)


def _qprep_call(cq, wq, wqs, wuk, cos_t, sin_t, layer, nseq, t, g_sz, tt, q_dtype):
    m = g_sz * tt
    nj = t // tt
    qw = A_HEADS * Q_SLOT
    return pl.pallas_call(
        _qprep_kernel,
        out_shape=jax.ShapeDtypeStruct((nseq, A_HEADS, t, QK_CAT), q_dtype),
        grid=(nseq // g_sz, nj),
        in_specs=[
            pl.BlockSpec((m, Q_LORA), lambda i, j: (i * nj + j, 0)),
            pl.BlockSpec((None, Q_LORA, qw), lambda i, j: (layer, 0, 0)),
            pl.BlockSpec((None, Q_LORA, qw), lambda i, j: (layer, 0, 0)),
            pl.BlockSpec((None, A_HEADS, Q_SLOT, KV_LORA), lambda i, j: (layer, 0, 0, 0)),
            pl.BlockSpec((m, LANES), lambda i, j: (j, 0)),
            pl.BlockSpec((m, LANES), lambda i, j: (j, 0)),
        ],
        out_specs=pl.BlockSpec((g_sz, A_HEADS, tt, QK_CAT), lambda i, j: (i, 0, j, 0)),
        compiler_params=_cparams(2),
        name="mla_qprep",
    )(cq, wq, wqs, wuk, cos_t, sin_t)


def _fold_lanes(x, op):
    out = x[:, 0:LANES]
    for j in range(1, x.shape[1] // LANES):
        out = op(out, x[:, j * LANES:(j + 1) * LANES])
    return out


def _attn_prompt_kernel(q_ref, k_ref, o_ref, s_ref, p_ref, m_ref, al_ref, l_ref, acc_ref, *, tk):
    qi = pl.program_id(1)
    tq = q_ref.shape[2]
    rows = A_HEADS * tq
    hp = A_HEADS // ATT_PARTS
    part_rows = hp * tq
    rc = ATT_ROW_CHUNK
    n_lane_groups = tk // LANES
    m_ref[...] = jnp.full_like(m_ref, NEG_INF)
    l_ref[...] = jnp.zeros_like(l_ref)
    acc_ref[...] = jnp.zeros_like(acc_ref)
    col_minus_row = (lax.broadcasted_iota(jnp.int32, (rc, tk), 1)
                     - lax.broadcasted_iota(jnp.int32, (rc, tk), 0))

    def block(kb, masked, n_groups):
        nk = n_groups * LANES
        k = k_ref[pl.ds(pl.multiple_of(kb * tk, tk), nk), :]
        kv = k[:, 0:KV_LORA]

        def scores(b):
            qh = q_ref[0, b * hp:(b + 1) * hp].reshape(part_rows, QK_CAT)
            s_ref[b * part_rows:(b + 1) * part_rows, 0:nk] = _bdot_nt(qh, k)

        def softmax(b):
            chunks = range(b * part_rows // rc, (b + 1) * part_rows // rc)
            for c in chunks:
                r = slice(c * rc, (c + 1) * rc)
                sc = s_ref[r, 0:nk]
                if masked:
                    thr = qi * tq - kb * tk + (c * rc) % tq
                    sc = jnp.where(col_minus_row[:, 0:nk] <= thr, sc, NEG_INF)
                    s_ref[r, 0:nk] = sc
                m_prev = m_ref[r, :]
                m_new = jnp.maximum(m_prev, jnp.max(_fold_lanes(sc, jnp.maximum), axis=1, keepdims=True))
                al_ref[r, :] = jnp.exp2(m_prev - m_new)
                m_ref[r, :] = m_new
            for c in chunks:
                r = slice(c * rc, (c + 1) * rc)
                m_new = m_ref[r, :]
                part = None
                for j in range(n_groups):
                    cols = slice(j * LANES, (j + 1) * LANES)
                    p = jnp.exp2(s_ref[r, cols] - m_new)
                    p_ref[r, cols] = p.astype(BF16)
                    part = p if part is None else part + p
                l_ref[r, :] = al_ref[r, :] * l_ref[r, :] + part

        def values(b):
            r = slice(b * part_rows, (b + 1) * part_rows)
            pv = _bdot(p_ref[r, 0:nk], kv)
            alpha = al_ref[r, :]
            for j in range(KV_LORA // LANES):
                cols = slice(j * LANES, (j + 1) * LANES)
                acc_ref[r, cols] = alpha * acc_ref[r, cols] + pv[:, cols]

        for step in range(ATT_PARTS + 2):
            if step < ATT_PARTS:
                scores(step)
            if 0 <= step - 1 < ATT_PARTS:
                softmax(step - 1)
            if 0 <= step - 2 < ATT_PARTS:
                values(step - 2)

    n_kb = (qi * tq + tq + tk - 1) // tk

    def body(kb, carry):
        block(kb, False, n_lane_groups)
        return carry

    lax.fori_loop(0, n_kb - 1, body, 0)
    tiles_per_block = tk // tq
    for v in range(tiles_per_block):
        @pl.when(qi % tiles_per_block == v)
        def _(v=v):
            block(n_kb - 1, True, ((v + 1) * tq + LANES - 1) // LANES)

    l_row = jnp.sum(l_ref[...], axis=1, keepdims=True)
    o_ref[0] = (acc_ref[...] / l_row).reshape(A_HEADS, tq, KV_LORA).astype(o_ref.dtype)


def _attn_prompt_call(q, kcat3):
    nseq, _, t, _ = q.shape
    tq = min(ATT_TQ, t)
    tk = min(ATT_TK, t)
    assert tk % tq == 0 and tq % ATT_ROW_CHUNK == 0 and t % tk == 0
    rows = A_HEADS * tq
    return pl.pallas_call(
        functools.partial(_attn_prompt_kernel, tk=tk),
        out_shape=jax.ShapeDtypeStruct((nseq, A_HEADS, t, KV_LORA), BF16),
        grid=(nseq, t // tq),
        in_specs=[pl.BlockSpec((1, A_HEADS, tq, QK_CAT), lambda i, j: (i, 0, j, 0)),
                  pl.BlockSpec((None, t, QK_CAT), lambda i, j: (i, 0, 0))],
        out_specs=pl.BlockSpec((1, A_HEADS, tq, KV_LORA), lambda i, j: (i, 0, j, 0)),
        scratch_shapes=[pltpu.VMEM((rows, tk), F32), pltpu.VMEM((rows, tk), BF16),
                        pltpu.VMEM((rows, LANES), F32), pltpu.VMEM((rows, LANES), F32),
                        pltpu.VMEM((rows, LANES), F32), pltpu.VMEM((rows, KV_LORA), F32)],
        compiler_params=_cparams(2),
        name="attn_prompt",
    )(q, kcat3)


def _attn_sample_kernel(pt_ref, q_ref, knew_ref, lat_hbm, krt_hbm, o_ref, latbuf, krtbuf, kbuf, krbuf, sems,
                        *, layer, n_pages):
    step = pl.program_id(0)
    n_seq = pl.num_programs(0)
    slot = step % 2
    nxt = jnp.minimum(step + 1, n_seq - 1)
    dec = q_ref.shape[2]
    rows = A_HEADS * dec

    def page_copies(seq, slot_):
        copies = []
        for i in range(n_pages):
            page = 0 if seq is None else pt_ref[seq, i]
            keys = pl.ds(i * PAGE_SIZE, PAGE_SIZE)
            copies.append(pltpu.make_async_copy(lat_hbm.at[layer, page], latbuf.at[slot_, keys, :],
                                                sems.at[0, slot_]))
            copies.append(pltpu.make_async_copy(krt_hbm.at[layer, page], krtbuf.at[slot_, :, keys],
                                                sems.at[1, slot_]))
        return copies

    @pl.when(step == 0)
    def _():
        for c in page_copies(0, 0):
            c.start()

    for c in page_copies(nxt, 1 - slot):
        c.start()
    for c in page_copies(None, slot):
        c.wait()

    for i in range(n_pages):
        keys = slice(i * PAGE_SIZE, (i + 1) * PAGE_SIZE)
        kbuf[keys, :] = latbuf[slot, keys, :].astype(BF16)
        krbuf[:, keys] = krtbuf[slot, :, keys].astype(BF16)
    q = q_ref[0].reshape(rows, QK_CAT)
    q_lat = q[:, 0:KV_LORA].astype(BF16)
    q_rope = q[:, KV_LORA + SM_KR:KV_LORA + SM_KR + QK_ROPE].astype(BF16)
    kv = kbuf[...]
    s = _bdot_nt(q_lat, kv) + _bdot(q_rope, krbuf[...])
    knew = knew_ref[0].astype(BF16)
    s2 = _bdot_nt(q.astype(BF16), knew)
    row = lax.broadcasted_iota(jnp.int32, s2.shape, 0)
    col = lax.broadcasted_iota(jnp.int32, s2.shape, 1)
    s2 = jnp.where(col <= (row & (dec - 1)), s2, NEG_INF)
    m = jnp.maximum(jnp.max(s, axis=1, keepdims=True), jnp.max(s2, axis=1, keepdims=True))
    p = jnp.exp2(s - m)
    p2 = jnp.exp2(s2 - m)
    l = jnp.sum(p, axis=1, keepdims=True) + jnp.sum(p2, axis=1, keepdims=True)
    o = _bdot(p.astype(BF16), kv) + _bdot(p2.astype(BF16), knew[:, 0:KV_LORA])
    o_ref[0] = (o / l).reshape(A_HEADS, dec, KV_LORA)

    @pl.when(step == n_seq - 1)
    def _():
        for c in page_copies(None, 1 - slot):
            c.wait()


def _attn_sample_call(page_table, q, kcat3, cache_latent, cache_krope_t, layer):
    nseq, _, dec, _ = q.shape
    n_pages = page_table.shape[1]
    n_keys = n_pages * PAGE_SIZE
    assert dec & (dec - 1) == 0
    grid_spec = pltpu.PrefetchScalarGridSpec(
        num_scalar_prefetch=1,
        grid=(nseq,),
        in_specs=[pl.BlockSpec((1, A_HEADS, dec, QK_CAT), lambda s, pt: (s, 0, 0, 0)),
                  pl.BlockSpec((1, dec, QK_CAT), lambda s, pt: (s, 0, 0)),
                  pl.BlockSpec(memory_space=pl.ANY),
                  pl.BlockSpec(memory_space=pl.ANY)],
        out_specs=pl.BlockSpec((1, A_HEADS, dec, KV_LORA), lambda s, pt: (s, 0, 0, 0)),
        scratch_shapes=[pltpu.VMEM((2, n_keys, KV_LORA), F32),
                        pltpu.VMEM((2, QK_ROPE, n_keys), F32),
                        pltpu.VMEM((n_keys, KV_LORA), BF16),
                        pltpu.VMEM((QK_ROPE, n_keys), BF16),
                        pltpu.SemaphoreType.DMA((2, 2))],
    )
    return pl.pallas_call(
        functools.partial(_attn_sample_kernel, layer=layer, n_pages=n_pages),
        out_shape=jax.ShapeDtypeStruct((nseq, A_HEADS, dec, KV_LORA), F32),
        grid_spec=grid_spec,
        compiler_params=_cparams(1),
        name="attn_sample",
    )(page_table, q, kcat3, cache_latent, cache_krope_t)


def _merge_kernel(h_ref, g2_ref, ac_ref, am_ref, ol_ref, gates_ref, wc_ref, wm_ref, wuv_ref, wo_ref, wmix_ref,
                  o_ref):
    g_sz, tt, _ = h_ref.shape
    m = g_sz * tt
    y_conv = _bdot(ac_ref[...].astype(BF16), wc_ref[...])
    y_m = _bdot(am_ref[...].astype(BF16), wm_ref[...])
    parts = []
    for h in range(A_HEADS):
        ol = ol_ref[:, h].reshape(m, KV_LORA).astype(BF16)
        parts.append(_bdot(ol, wuv_ref[h]))
    o = jnp.concatenate(parts, axis=-1).astype(BF16)
    y_a = _bdot(o, wo_ref[...])
    gate = lambda b: gates_ref[:, b * D_MODEL:(b + 1) * D_MODEL].astype(F32)
    merged = gate(0) * y_conv + gate(1) * y_m + gate(2) * y_a
    mixed = _bdot(merged.astype(BF16), wmix_ref[...])
    o_ref[...] = h_ref[...] + g2_ref[...] * mixed.reshape(g_sz, tt, D_MODEL)


def _merge_call(h, mods, layer, a_conv, a_m, o_lat, gates, wc, wm, wuv, wo, wmix, g_sz, tt):
    nseq, t, _ = h.shape
    m = g_sz * tt
    nj = t // tt
    rows = lambda w: pl.BlockSpec((m, w), lambda i, j: (i * nj + j, 0))
    row3 = pl.BlockSpec((g_sz, tt, D_MODEL), lambda i, j: (i, j, 0))
    wspec = lambda a, b: pl.BlockSpec((None, a, b), lambda i, j: (layer, 0, 0))
    return pl.pallas_call(
        _merge_kernel,
        out_shape=jax.ShapeDtypeStruct(h.shape, F32),
        grid=(nseq // g_sz, nj),
        in_specs=[
            row3,
            _mod_spec(layer, 5, g_sz),
            rows(CONV_WIDTH), rows(M_WIDTH),
            pl.BlockSpec((g_sz, A_HEADS, tt, KV_LORA), lambda i, j: (i, 0, j, 0)),
            rows(3 * D_MODEL),
            wspec(CONV_WIDTH, D_MODEL), wspec(M_WIDTH, D_MODEL),
            pl.BlockSpec((None, A_HEADS, KV_LORA, V_HEAD), lambda i, j: (layer, 0, 0, 0)),
            wspec(A_HEADS * V_HEAD, D_MODEL), wspec(D_MODEL, D_MODEL),
        ],
        out_specs=row3,
        compiler_params=_cparams(2),
        name="merge",
    )(h, mods, a_conv, a_m, o_lat, gates, wc, wm, wuv, wo, wmix)


def _rope_tables(pos):
    half = QK_ROPE // 2
    inv = ROPE_BASE ** (-jnp.arange(half, dtype=F32) * 2.0 / QK_ROPE)
    ang = pos.astype(F32)[:, None] * inv[None, :]
    cos, sin = jnp.cos(ang), jnp.sin(ang)
    n = pos.shape[0]
    ones = jnp.ones((n, SM_KR), F32)
    tail = LANES - SM_KR - QK_ROPE
    cos_t = jnp.concatenate([ones, cos, cos, jnp.ones((n, tail), F32)], axis=1)
    sin_t = jnp.concatenate([jnp.zeros((n, SM_KR), F32), -sin, sin, jnp.zeros((n, tail), F32)], axis=1)
    return cos_t, sin_t


def _rearranged_w_in(w_in):
    depth = w_in.shape[0]
    off = np.cumsum((0,) + (CONV_WIDTH,) * 3 + (M_WIDTH,) * 4 + (M_HEADS,) * 2
                    + (Q_LORA, KV_LORA, QK_ROPE) + (D_MODEL,) * 3)
    zi = w_in[:, :, off[7]:off[8]]
    zf = w_in[:, :, off[8]:off[9]]
    zdq = w_in[:, :, off[9]:off[10]]
    zdkv = w_in[:, :, off[10]:off[11]]
    zkr = w_in[:, :, off[11]:off[12]]
    gates = w_in[:, :, off[12]:off[15]]
    zeros = lambda n: jnp.zeros((depth, D_MODEL, n), w_in.dtype)
    half = QK_ROPE // 2
    small_a = jnp.concatenate([zi, zf, zeros(SM_KR - 2 * M_HEADS), zkr, zeros(LANES - SM_KR - QK_ROPE)], axis=2)
    small_b = jnp.concatenate([zeros(SM_KR), zkr[:, :, half:], zkr[:, :, :half],
                               zeros(LANES - SM_KR - QK_ROPE)], axis=2)
    return jnp.concatenate([w_in[:, :, 0:off[7]], zdq, zdkv, gates, small_a, small_b], axis=2).astype(BF16)


def _rearranged_w_uq(w_uq):
    depth = w_uq.shape[0]
    per = QK_NOPE + QK_ROPE
    half = QK_ROPE // 2
    w = w_uq.reshape(depth, Q_LORA, A_HEADS, per)
    nope = w[..., :QK_NOPE]
    r1 = w[..., QK_NOPE:QK_NOPE + half]
    r2 = w[..., QK_NOPE + half:]
    pad = jnp.zeros((depth, Q_LORA, A_HEADS, Q_SLOT - per), w_uq.dtype)
    wq = jnp.concatenate([nope, r1, r2, pad], axis=-1)
    wqs = jnp.concatenate([jnp.zeros_like(nope), r2, r1, pad], axis=-1)
    flat = lambda a: a.reshape(depth, Q_LORA, A_HEADS * Q_SLOT).astype(BF16)
    return flat(wq), flat(wqs)


def _block_diag_state(c):
    nseq = c.shape[0]
    eye = jnp.eye(M_HEADS, dtype=c.dtype)
    return jnp.einsum('bhde,hg->bhdge', c, eye).reshape(nseq, M_HEADS * M_DK, M_HEADS * M_DV)


def _diag_blocks(c_bd):
    nseq = c_bd.shape[0]
    c5 = c_bd.reshape(nseq, M_HEADS, M_DK, M_HEADS, M_DV)
    return jnp.stack([c5[:, h, :, h, :] for h in range(M_HEADS)], axis=1)


def _tile_plan(nseq, t):
    if t >= MLSTM_CHUNK_P:
        blk = lambda rows: (1, min(rows, t))
        act = BF16
    else:
        blk = lambda rows: (min(max(rows // t, 1), nseq), t)
        act = F32
    plan = dict(ffn=blk(1024), inproj=blk(256), conv=blk(1024 if act == F32 else 512), qprep=blk(512),
                merge=blk(512), mlstm=(min(MLSTM_SEQS_P if act == BF16 else MLSTM_SEQS_S, nseq), min(MLSTM_CHUNK_P, t)),
                act=act)
    for name in ('ffn', 'inproj', 'conv', 'qprep', 'merge', 'mlstm'):
        g_sz, tt = plan[name]
        assert nseq % g_sz == 0 and t % tt == 0, (name, nseq, t)
    return plan


def _layer_group(x, mods, layer, wts, conv_prev, c0, n0, m0, cos_t, sin_t, attend, final):
    nseq, t, _ = x.shape
    plan = _tile_plan(nseq, t)
    act_dtype = plan['act']
    h = _ffn_call(x, mods, layer, True, wts['ln_ffn1'], wts['ffn1_w13'], wts['ffn1_w2'], wts['final_norm'],
                  False, *plan['ffn'])
    za, cq, ckv, kcat, gates, small = _inproj_call(
        h, mods, layer, wts['ln_mix'], wts['w_in_r'], wts['mla_q_norm'], wts['mla_kv_norm'],
        cos_t, sin_t, wts['brow'], *plan['inproj'], act_dtype)
    za3 = za.reshape(nseq, t, ZA_W)
    small3 = small.reshape(nseq, t, LANES)
    a_conv, conv_new = _conv_call(za3, conv_prev, wts['conv_w'][layer], *plan['conv'])
    gates_t = jnp.swapaxes(small3[:, :, 0:2 * M_HEADS], 1, 2)
    a_m, c_bd, n1, m1 = _mlstm_call(za3, small3, gates_t, c0, n0, m0, wts['mlstm_norm'][layer], *plan['mlstm'],
                                    act_dtype)
    a_m = a_m.reshape(nseq * t, M_WIDTH)
    q = _qprep_call(cq, wts['wq'], wts['wqs'], wts['wuk'], cos_t, sin_t, layer, nseq, t, *plan['qprep'], act_dtype)
    o_lat = attend(q, kcat.reshape(nseq, t, QK_CAT))
    h2 = _merge_call(h, mods, layer, a_conv, a_m, o_lat, gates, wts['conv_out'], wts['mlstm_out'], wts['wuv'],
                     wts['mla_w_o'], wts['w_mix_out'], *plan['merge'])
    out = _ffn_call(h2, mods, layer, False, wts['ln_ffn2'], wts['ffn2_w13'], wts['ffn2_w2'], wts['final_norm'],
                    final, *plan['ffn'])
    state = (ckv.reshape(nseq, t, KV_LORA), small3[:, :, SM_KR:SM_KR + QK_ROPE], conv_new,
             _diag_blocks(c_bd), n1.reshape(nseq, M_HEADS, M_DK), m1[:, 0, 0:M_HEADS])
    return out, state


def kernel(x_prompt, x_sample, cache_latent, cache_krope, state_conv, state_mlstm_C, state_mlstm_n, state_mlstm_m, page_table, c_prompt, c_sample, ln_ffn1, ffn1_w13, ffn1_w2, ln_mix, w_in, conv_w, conv_out, mlstm_b_i, mlstm_b_f, mlstm_norm, mlstm_out, mla_q_norm, mla_w_uq, mla_w_uk, mla_kv_norm, mla_w_uv, mla_w_o, w_mix_out, ln_ffn2, ffn2_w13, ffn2_w2, w_ada, b_ada, final_norm):
    bp, seq, _ = x_prompt.shape
    bs, dec, _ = x_sample.shape
    past_len = page_table.shape[1] * PAGE_SIZE

    c_all = jnp.concatenate([c_prompt, c_sample], axis=0)
    mods = _mods_call(c_all, w_ada, b_ada)
    mods_p = mods[:, :, :bp].reshape(DEPTH, 9, bp, 1, D_MODEL)
    mods_s = mods[:, :, bp:].reshape(DEPTH, 9, bs, 1, D_MODEL)

    wq, wqs = _rearranged_w_uq(mla_w_uq)
    wuk = mla_w_uk.reshape(DEPTH, KV_LORA, A_HEADS, QK_NOPE).transpose(0, 2, 3, 1)
    wuk = jnp.pad(wuk, ((0, 0), (0, 0), (0, Q_SLOT - QK_NOPE), (0, 0))).astype(BF16)
    brow = jnp.concatenate([mlstm_b_i, mlstm_b_f, jnp.zeros((DEPTH, LANES - 2 * M_HEADS), F32)], axis=1)
    wts = {
        'ln_ffn1': ln_ffn1.reshape(DEPTH, 1, D_MODEL), 'ffn1_w13': ffn1_w13.astype(BF16), 'ffn1_w2': ffn1_w2.astype(BF16),
        'ln_mix': ln_mix.reshape(DEPTH, 1, D_MODEL), 'w_in_r': _rearranged_w_in(w_in), 'conv_w': conv_w,
        'conv_out': conv_out.astype(BF16), 'brow': brow.reshape(DEPTH, 1, LANES),
        'mlstm_norm': mlstm_norm.reshape(DEPTH, 1, M_WIDTH), 'mlstm_out': mlstm_out.astype(BF16),
        'mla_q_norm': mla_q_norm.reshape(DEPTH, 1, Q_LORA), 'wq': wq, 'wqs': wqs, 'wuk': wuk, 'mla_kv_norm': mla_kv_norm.reshape(DEPTH, 1, KV_LORA),
        'wuv': mla_w_uv.reshape(DEPTH, KV_LORA, A_HEADS, V_HEAD).transpose(0, 2, 1, 3).astype(BF16),
        'mla_w_o': mla_w_o.astype(BF16), 'w_mix_out': w_mix_out.astype(BF16),
        'ln_ffn2': ln_ffn2.reshape(DEPTH, 1, D_MODEL), 'ffn2_w13': ffn2_w13.astype(BF16), 'ffn2_w2': ffn2_w2.astype(BF16),
        'final_norm': final_norm.reshape(1, D_MODEL),
    }

    cos_p, sin_p = _rope_tables(jnp.arange(seq))
    cos_s, sin_s = _rope_tables(past_len + jnp.arange(dec))
    cos_s = jnp.tile(cos_s, (bs, 1))
    sin_s = jnp.tile(sin_s, (bs, 1))

    cache_krope_t = jnp.swapaxes(cache_krope, 2, 3)

    zero_conv = jnp.zeros((bp, CONV_K - 1, CONV_WIDTH), F32)
    zero_c = jnp.zeros((bp, M_WIDTH, M_WIDTH), F32)
    zero_n = jnp.zeros((bp, 1, M_WIDTH), F32)
    zero_m = jnp.zeros((bp, 1, LANES), F32)

    xp, xs = x_prompt, x_sample
    p_st, s_st = [], []
    for layer in range(DEPTH):
        final = layer == DEPTH - 1
        xp, sp = _layer_group(xp, mods_p, layer, wts, zero_conv, zero_c, zero_n, zero_m,
                              cos_p, sin_p, _attn_prompt_call, final)
        p_st.append(sp)
        m0 = jnp.pad(state_mlstm_m[layer], ((0, 0), (0, LANES - M_HEADS))).reshape(bs, 1, LANES)
        attend_s = functools.partial(_attn_sample_call, page_table, cache_latent=cache_latent,
                                     cache_krope_t=cache_krope_t, layer=layer)
        xs, ss = _layer_group(xs, mods_s, layer, wts, state_conv[layer],
                              _block_diag_state(state_mlstm_C[layer]),
                              state_mlstm_n[layer].reshape(bs, 1, M_WIDTH), m0,
                              cos_s, sin_s, attend_s, final)
        s_st.append(ss)
    p_out = tuple(jnp.stack(a) for a in zip(*p_st))
    s_out = tuple(jnp.stack(a) for a in zip(*s_st))
    return (xp, xs) + p_out + s_out
```
